```python
import jax
import jax.numpy as jnp
from jax import lax
import numpy as np

D_MODEL = 1024
BATCH = 4
SEQ = 8192
DEPTH = 2

MEM_LEN = 256
EPS = 1e-6
N_BRANCH = 3

SSD_WIDTH = 2 * D_MODEL
SSD_HEAD_DIM = 64
SSD_HEADS = SSD_WIDTH // SSD_HEAD_DIM
SSD_GROUPS = 8
SSD_STATE = 128
SSD_CONV = 5
SSD_CHUNK = 128
SSD_CONV_DIM = SSD_WIDTH + 2 * SSD_GROUPS * SSD_STATE

GMLP_WIDTH = D_MODEL
GMLP_GROUPS = 8
GMLP_CHUNK = 128

XATTN_HEADS = 4
XATTN_HEAD_DIM = D_MODEL // XATTN_HEADS
XATTN_WIDTH = XATTN_HEADS * XATTN_HEAD_DIM

IN_COLS = SSD_WIDTH + SSD_CONV_DIM + 2 * SSD_HEADS + 3 * GMLP_WIDTH + 2 * XATTN_WIDTH + N_BRANCH * D_MODEL

kernel_name = 'hybrid_ssd_gmlp_memxattn_encoder'


def _rmsnorm(x, g):
    xf = x.astype(jnp.float32)
    y = xf * lax.rsqrt(jnp.mean(xf * xf, axis=-1, keepdims=True) + EPS)
    return (y * g.astype(jnp.float32)).astype(x.dtype)


def _group_rmsnorm(x, g, groups):
    b, L, w = x.shape
    xf = x.astype(jnp.float32).reshape(b, L, groups, w // groups)
    y = xf * lax.rsqrt(jnp.mean(xf * xf, axis=-1, keepdims=True) + EPS)
    return (y.reshape(b, L, w) * g.astype(jnp.float32)).astype(x.dtype)


def _layernorm(x, g, bias):
    xf = x.astype(jnp.float32)
    mu = jnp.mean(xf, axis=-1, keepdims=True)
    xc = xf - mu
    y = xc * lax.rsqrt(jnp.mean(xc * xc, axis=-1, keepdims=True) + EPS)
    return (y * g.astype(jnp.float32) + bias.astype(jnp.float32)).astype(x.dtype)


def _dwconv_centred(x, w, bias):
    c = x.shape[-1]
    y = lax.conv_general_dilated(
        x, w[:, None, :].astype(x.dtype), window_strides=(1,),
        padding=[(SSD_CONV // 2, SSD_CONV // 2)],
        dimension_numbers=('NWC', 'WIO', 'NWC'), feature_group_count=c)
    return y + bias.astype(x.dtype)


def _ssd_scan(xh, dt, a, bm, cm):
    b, L, H, P = xh.shape
    G, N = bm.shape[2], bm.shape[3]
    R = H // G
    Q = SSD_CHUNK
    nc = L // Q
    xdt = (xh.astype(jnp.float32) * dt[..., None]).reshape(b, nc, Q, G, R, P)
    bc = bm.astype(jnp.float32).reshape(b, nc, Q, G, N)
    cc = cm.astype(jnp.float32).reshape(b, nc, Q, G, N)
    cs = jnp.cumsum((dt * a).reshape(b, nc, Q, G, R), axis=2)
    lower = jnp.tril(jnp.ones((Q, Q), dtype=bool))[:, :, None, None]
    seg = cs[:, :, :, None] - cs[:, :, None, :]
    decay = jnp.exp(jnp.where(lower, seg, -jnp.inf))
    scores = jnp.einsum('bctgn,bcsgn->bctsg', cc, bc)
    y_diag = jnp.einsum('bctsgr,bcsgrp->bctgrp', scores[..., None] * decay, xdt)
    to_end = jnp.exp(cs[:, :, -1:] - cs)
    states = jnp.einsum('bcsgn,bcsgrp->bcgrpn', bc, xdt * to_end[..., None])
    chunk_decay = jnp.exp(cs[:, :, -1])

    def step(h, inp):
        s, d = inp
        return h * d[..., None, None] + s, h

    h0 = jnp.zeros((b, G, R, P, N), jnp.float32)
    _, h_prev = lax.scan(step, h0, (jnp.moveaxis(states, 1, 0), jnp.moveaxis(chunk_decay, 1, 0)))
    h_prev = jnp.moveaxis(h_prev, 0, 1)
    y_off = jnp.einsum('bctgn,bcgrpn->bctgrp', cc, h_prev) * jnp.exp(cs)[..., None]
    return (y_diag + y_off).reshape(b, L, H, P)


def _ssd_branch(z, xbc, dt_raw, conv_w, conv_b, dt_bias, a_log, d_skip, norm_g):
    b, L, _ = z.shape
    xbc = jax.nn.silu(_dwconv_centred(xbc, conv_w, conv_b))
    xs, bm, cm = jnp.split(xbc, [SSD_WIDTH, SSD_WIDTH + SSD_GROUPS * SSD_STATE], axis=-1)
    xh = xs.reshape(b, L, SSD_HEADS, SSD_HEAD_DIM)
    bm = bm.reshape(b, L, SSD_GROUPS, SSD_STATE)
    cm = cm.reshape(b, L, SSD_GROUPS, SSD_STATE)
    dt = jax.nn.softplus(dt_raw.astype(jnp.float32).reshape(b, L, 2, SSD_HEADS) + dt_bias.astype(jnp.float32))
    a = -jnp.exp(a_log.astype(jnp.float32))
    y_fwd = _ssd_scan(xh, dt[:, :, 0], a[0], bm, cm)
    y_bwd = jnp.flip(_ssd_scan(jnp.flip(xh, 1), jnp.flip(dt[:, :, 1], 1), a[1],
                               jnp.flip(bm, 1), jnp.flip(cm, 1)), 1)
    y = y_fwd + y_bwd + d_skip.astype(jnp.float32)[:, None] * xh.astype(jnp.float32)
    y = y.reshape(b, L, SSD_WIDTH).astype(z.dtype)
    return _group_rmsnorm(y * jax.nn.silu(z), norm_g, SSD_GROUPS)


def _gmlp_branch(gate, uv, ln_g, ln_b, w_s, b_s):
    b, L, _ = gate.shape
    u, v = jnp.split(jax.nn.gelu(uv), 2, axis=-1)
    v = _layernorm(v, ln_g, ln_b)
    vc = v.reshape(b, L // GMLP_CHUNK, GMLP_CHUNK, GMLP_GROUPS, GMLP_WIDTH // GMLP_GROUPS)
    sv = jnp.einsum('gts,bcsgd->bctgd', w_s, vc) + b_s.T[:, :, None]
    return u * sv.reshape(b, L, GMLP_WIDTH) * jax.nn.silu(gate)


def _xattn_branch(q, gate, mem_n, w_kv):
    b, L, _ = q.shape
    k, v = jnp.split(jnp.einsum('bmd,de->bme', mem_n, w_kv), 2, axis=-1)
    qh = q.reshape(b, L, XATTN_HEADS, XATTN_HEAD_DIM)
    kh = k.reshape(b, MEM_LEN, XATTN_HEADS, XATTN_HEAD_DIM)
    vh = v.reshape(b, MEM_LEN, XATTN_HEADS, XATTN_HEAD_DIM)
    s = jnp.einsum('bqhd,bkhd->bhqk', qh, kh).astype(jnp.float32) * (XATTN_HEAD_DIM ** -0.5)
    p = jax.nn.softmax(s, axis=-1).astype(vh.dtype)
    o = jnp.einsum('bhqk,bkhd->bqhd', p, vh).reshape(b, L, XATTN_WIDTH)
    return o * jax.nn.silu(gate)


def setup_inputs(seed: int = 0) -> dict:
    key = jax.random.key(seed)
    ks = jax.random.split(key, 24)
    f32 = jnp.float32

    def nrm(k, shape, scale):
        return jax.random.normal(k, shape, f32) * scale

    def gain(k, shape):
        return 1.0 + 0.02 * jax.random.normal(k, shape, f32)

    dt0 = jnp.exp(jax.random.uniform(ks[6], (DEPTH, 2, SSD_HEADS), f32, np.log(1e-3), np.log(1e-1)))
    dt_bias = dt0 + jnp.log(-jnp.expm1(-dt0))
    a_log = jnp.log(jax.random.uniform(ks[7], (DEPTH, 2, SSD_HEADS), f32, 1.0, 16.0))
    return {
        'x': nrm(ks[0], (BATCH, SEQ, D_MODEL), 1.0),
        'mem': nrm(ks[1], (BATCH, MEM_LEN, D_MODEL), 1.0),
        'norm_pre_g': gain(ks[2], (DEPTH, D_MODEL)),
        'w_in': nrm(ks[3], (DEPTH, D_MODEL, IN_COLS), D_MODEL ** -0.5),
        'conv_w': nrm(ks[4], (DEPTH, SSD_CONV, SSD_CONV_DIM), SSD_CONV ** -0.5),
        'conv_b': nrm(ks[5], (DEPTH, SSD_CONV_DIM), 0.01),
        'dt_bias': dt_bias,
        'a_log': a_log,
        'd_skip': gain(ks[8], (DEPTH, SSD_HEADS)),
        'ssd_norm_g': gain(ks[9], (DEPTH, SSD_WIDTH)),
        'gmlp_ln_g': gain(ks[10], (DEPTH, GMLP_WIDTH)),
        'gmlp_ln_b': nrm(ks[11], (DEPTH, GMLP_WIDTH), 0.01),
        'w_spatial': nrm(ks[12], (DEPTH, GMLP_GROUPS, GMLP_CHUNK, GMLP_CHUNK), GMLP_CHUNK ** -0.5),
        'b_spatial': gain(ks[13], (DEPTH, GMLP_GROUPS, GMLP_CHUNK)),
        'mem_norm_g': gain(ks[14], (DEPTH, D_MODEL)),
        'w_kv': nrm(ks[15], (DEPTH, D_MODEL, 2 * XATTN_WIDTH), D_MODEL ** -0.5),
        'w_br_ssd': nrm(ks[16], (DEPTH, SSD_WIDTH, D_MODEL), SSD_WIDTH ** -0.5),
        'w_br_gmlp': nrm(ks[17], (DEPTH, GMLP_WIDTH, D_MODEL), GMLP_WIDTH ** -0.5),
        'w_br_xattn': nrm(ks[18], (DEPTH, XATTN_WIDTH, D_MODEL), XATTN_WIDTH ** -0.5),
        'w_out': nrm(ks[19], (DEPTH, D_MODEL, D_MODEL), D_MODEL ** -0.5),
        'norm_post_g': gain(ks[20], (DEPTH, D_MODEL)),
    }


def reference(x, mem, norm_pre_g, w_in, conv_w, conv_b, dt_bias, a_log, d_skip,
              ssd_norm_g, gmlp_ln_g, gmlp_ln_b, w_spatial, b_spatial, mem_norm_g,
              w_kv, w_br_ssd, w_br_gmlp, w_br_xattn, w_out, norm_post_g):
    b, L, _ = x.shape
    sizes = [SSD_WIDTH, SSD_CONV_DIM, 2 * SSD_HEADS, GMLP_WIDTH, 2 * GMLP_WIDTH,
             XATTN_WIDTH, XATTN_WIDTH, N_BRANCH * D_MODEL]
    split_points = np.cumsum(sizes)[:-1].tolist()
    for l in range(DEPTH):
        h = _rmsnorm(x, norm_pre_g[l])
        proj = jnp.einsum('bsd,de->bse', h, w_in[l])
        z, xbc, dt_raw, g_gate, g_uv, xq, x_gate, merge = jnp.split(proj, split_points, axis=-1)
        y_ssd = _ssd_branch(z, xbc, dt_raw, conv_w[l], conv_b[l], dt_bias[l], a_log[l],
                            d_skip[l], ssd_norm_g[l])
        y_gmlp = _gmlp_branch(g_gate, g_uv, gmlp_ln_g[l], gmlp_ln_b[l], w_spatial[l], b_spatial[l])
        y_xattn = _xattn_branch(xq, x_gate, _rmsnorm(mem, mem_norm_g[l]), w_kv[l])
        gates = jax.nn.sigmoid(merge.astype(jnp.float32)).astype(x.dtype).reshape(b, L, N_BRANCH, D_MODEL)
        merged = (gates[:, :, 0] * jnp.einsum('bse,ed->bsd', y_ssd, w_br_ssd[l])
                  + gates[:, :, 1] * jnp.einsum('bse,ed->bsd', y_gmlp, w_br_gmlp[l])
                  + gates[:, :, 2] * jnp.einsum('bse,ed->bsd', y_xattn, w_br_xattn[l]))
        out = jnp.einsum('bsd,de->bse', merged, w_out[l])
        x = x + _rmsnorm(out, norm_post_g[l])
    return x
```

```python
import functools

import jax
import jax.numpy as jnp
from jax import lax
from jax.experimental import pallas as pl
from jax.experimental.pallas import tpu as pltpu

F32 = jnp.float32
BF16 = jnp.bfloat16

EPS = 1e-6
D_MODEL = 1024
N_BRANCH = 3

SSD_WIDTH = 2 * D_MODEL
SSD_HEAD_DIM = 64
SSD_HEADS = SSD_WIDTH // SSD_HEAD_DIM
SSD_GROUPS = 8
SSD_STATE = 128
SSD_CONV = 5
SSD_CHUNK = 128
HEADS_PER_GROUP = SSD_HEADS // SSD_GROUPS
GROUP_WIDTH = HEADS_PER_GROUP * SSD_HEAD_DIM
DT_ROWS = 2 * HEADS_PER_GROUP

GMLP_WIDTH = D_MODEL
GMLP_GROUPS = 8
GMLP_CHUNK = 128
GMLP_GROUP_WIDTH = GMLP_WIDTH // GMLP_GROUPS

XATTN_HEADS = 4
XATTN_HEAD_DIM = D_MODEL // XATTN_HEADS
MEM_LEN = 256

LANES = 128
BF16_ROWS = 16

COL_BLOCK = 1024
BLK_Z, BLK_XS, BLK_BM, BLK_CM, BLK_GATE, BLK_U, BLK_V, BLK_XQ, BLK_XGATE, BLK_MERGE = 0, 2, 4, 5, 6, 7, 8, 9, 10, 11
N_COL_BLOCKS = 14
PROJ_COLS = N_COL_BLOCKS * COL_BLOCK
HALO = BF16_ROWS
CONV_PAD = SSD_CONV // 2


def _silu(x):
    return x * jax.nn.sigmoid(x)


def _gelu_tanh(x):
    c = 0.7978845608028654
    return x * (0.5 * (1.0 + jnp.tanh(c * (x + 0.044715 * (x * x * x)))))


def _rms(x, g):
    return x * lax.rsqrt(jnp.mean(x * x, axis=-1, keepdims=True) + EPS) * g


def _inproj_kernel(xprev_ref, x_ref, xnext_ref, gpre_ref, w_ref, wdth_ref, wdtl_ref,
                   convw_ref, convb_ref, lng_ref, lnb_ref,
                   out_ref, dt_ref, h_scr, acc_scr, *, tm, sub, tiles_per_seq):
    i = pl.program_id(0)
    j = pl.program_id(1)
    n_sub = tm // sub
    nt_dims = (((1,), (1,)), ((), ()))

    @pl.when(j == 0)
    def _():
        g = gpre_ref[...]
        first = (i % tiles_per_seq) == 0
        last = (i % tiles_per_seq) == tiles_per_seq - 1
        hp = _rms(xprev_ref[...], g) * jnp.where(first, 0.0, 1.0)
        hn = _rms(xnext_ref[...], g) * jnp.where(last, 0.0, 1.0)
        h_scr[0:HALO, :] = hp.astype(BF16)
        h_scr[HALO + tm:HALO + tm + HALO, :] = hn.astype(BF16)
        for s in range(n_sub):
            h = _rms(x_ref[s * sub:(s + 1) * sub, :], g)
            hb = h.astype(BF16)
            h_scr[HALO + s * sub:HALO + (s + 1) * sub, :] = hb
            hl = (h - hb.astype(F32)).astype(BF16)
            dt = (lax.dot_general(wdth_ref[...], hb, nt_dims, preferred_element_type=F32)
                  + lax.dot_general(wdth_ref[...], hl, nt_dims, preferred_element_type=F32)
                  + lax.dot_general(wdtl_ref[...], hb, nt_dims, preferred_element_type=F32))
            for c in range(sub // SSD_CHUNK):
                dt_ref[s * (sub // SSD_CHUNK) + c] = dt[:, c * SSD_CHUNK:(c + 1) * SSD_CHUNK]

    def plain(act):
        for s in range(n_sub):
            rows = h_scr[HALO + s * sub:HALO + (s + 1) * sub, :]
            acc = jnp.dot(rows, w_ref[...], preferred_element_type=F32)
            out_ref[s * sub:(s + 1) * sub, :] = act(acc).astype(BF16)

    is_silu = (j == BLK_Z) | (j == BLK_Z + 1) | (j == BLK_GATE) | (j == BLK_XGATE)
    is_conv = (j >= BLK_XS) & (j <= BLK_CM)
    is_sig = j >= BLK_MERGE

    @pl.when(is_silu)
    def _():
        plain(_silu)

    @pl.when(j == BLK_XQ)
    def _():
        plain(lambda a: a)

    @pl.when(j == BLK_U)
    def _():
        plain(_gelu_tanh)

    @pl.when(j == BLK_V)
    def _():
        def gelu_ln(a):
            v = _gelu_tanh(a)
            mu = jnp.mean(v, axis=-1, keepdims=True)
            vc = v - mu
            return vc * lax.rsqrt(jnp.mean(vc * vc, axis=-1, keepdims=True) + EPS) * lng_ref[...] + lnb_ref[...]
        plain(gelu_ln)

    @pl.when(is_sig)
    def _():
        plain(jax.nn.sigmoid)

    @pl.when(is_conv)
    def _():
        for s in range(n_sub):
            rows = h_scr[s * sub:(s + 1) * sub + 2 * HALO, :]
            acc_scr[...] = jnp.dot(rows, w_ref[...], preferred_element_type=F32)
            y = convb_ref[...]
            for k in range(SSD_CONV):
                off = HALO - CONV_PAD + k
                y = y + convw_ref[k:k + 1, :] * acc_scr[off:off + sub, :]
            out_ref[s * sub:(s + 1) * sub, :] = _silu(y).astype(BF16)


def _inproj(x2, gpre, w_main, wdt_hi, wdt_lo, convw, convb, lng, lnb, *, seq_len, tm=1024, sub=512):
    t = x2.shape[0]
    assert t % tm == 0 and seq_len % tm == 0 and tm % sub == 0 and sub % SSD_CHUNK == 0
    n_tiles = t // tm
    halo_blocks = t // HALO
    per_tile = tm // HALO
    kernel = functools.partial(_inproj_kernel, tm=tm, sub=sub, tiles_per_seq=seq_len // tm)
    conv_idx = lambda i, j: (0, jnp.clip(j - BLK_XS, 0, BLK_CM - BLK_XS))
    return pl.pallas_call(
        kernel,
        grid=(n_tiles, N_COL_BLOCKS),
        in_specs=[
            pl.BlockSpec((HALO, D_MODEL), lambda i, j: (jnp.maximum(i * per_tile - 1, 0), 0)),
            pl.BlockSpec((tm, D_MODEL), lambda i, j: (i, 0)),
            pl.BlockSpec((HALO, D_MODEL), lambda i, j: (jnp.minimum((i + 1) * per_tile, halo_blocks - 1), 0)),
            pl.BlockSpec((1, D_MODEL), lambda i, j: (0, 0)),
            pl.BlockSpec((D_MODEL, COL_BLOCK), lambda i, j: (0, j)),
            pl.BlockSpec((2 * SSD_HEADS, D_MODEL), lambda i, j: (0, 0)),
            pl.BlockSpec((2 * SSD_HEADS, D_MODEL), lambda i, j: (0, 0)),
            pl.BlockSpec((SSD_CONV, COL_BLOCK), conv_idx),
            pl.BlockSpec((1, COL_BLOCK), conv_idx),
            pl.BlockSpec((1, COL_BLOCK), lambda i, j: (0, 0)),
            pl.BlockSpec((1, COL_BLOCK), lambda i, j: (0, 0)),
        ],
        out_specs=[
            pl.BlockSpec((tm, COL_BLOCK), lambda i, j: (i, j)),
            pl.BlockSpec((tm // SSD_CHUNK, 2 * SSD_HEADS, SSD_CHUNK), lambda i, j: (i, 0, 0)),
        ],
        out_shape=[
            jax.ShapeDtypeStruct((t, PROJ_COLS), BF16),
            jax.ShapeDtypeStruct((t // SSD_CHUNK, 2 * SSD_HEADS, SSD_CHUNK), F32),
        ],
        scratch_shapes=[
            pltpu.VMEM((tm + 2 * HALO, D_MODEL), BF16),
            pltpu.VMEM((sub + 2 * HALO, COL_BLOCK), F32),
        ],
        compiler_params=pltpu.CompilerParams(
            dimension_semantics=("parallel", "arbitrary"),
            vmem_limit_bytes=48 * 1024 * 1024),
        name="inproj",
    )(x2, x2, x2, gpre, w_main, wdt_hi, wdt_lo, convw, convb, lng, lnb)


TRI_BLOCKS = 5
EXPAND_BLOCKS = 4
COL_CS, COL_ECS, COL_W = 0, DT_ROWS, 2 * DT_ROWS


def _softplus(x):
    return jnp.maximum(x, 0.0) + jnp.log1p(jnp.exp(-jnp.abs(x)))


def _dt_rows(dt_raw, dt_bias, a, tri):
    dt = _softplus(dt_raw + dt_bias)
    da = dt * a
    hi = da.astype(BF16).astype(F32)
    r1 = da - hi
    mid = r1.astype(BF16).astype(F32)
    lo = r1 - mid
    parts = jnp.concatenate([hi, mid, lo, jnp.zeros_like(lo)], axis=0).astype(BF16)
    sums = jnp.dot(parts, tri, preferred_element_type=F32)
    s = sums[0:DT_ROWS] + sums[DT_ROWS:2 * DT_ROWS] + sums[2 * DT_ROWS:3 * DT_ROWS]
    q = SSD_CHUNK
    fwd = lax.broadcasted_iota(jnp.int32, (DT_ROWS, q), 0) < HEADS_PER_GROUP
    cs = jnp.where(fwd, s[:, 0:q], s[:, q:2 * q])
    rem = jnp.where(fwd, s[:, 2 * q:3 * q], s[:, 3 * q:4 * q])
    total = s[:, 4 * q:5 * q]
    return dt, cs, jnp.exp(cs), dt * jnp.exp(rem), jnp.exp(total)


def _token_columns(cs, ecs, w):
    pad = jnp.zeros((SSD_CHUNK - 3 * DT_ROWS, SSD_CHUNK), F32)
    return jnp.transpose(jnp.concatenate([cs, ecs, w, pad], axis=0))


def _head_row(v):
    lane = lax.broadcasted_iota(jnp.int32, (1, LANES), 1)
    out = []
    for d in range(2):
        r = d * HEADS_PER_GROUP
        lo = jnp.where(lane < SSD_HEAD_DIM, v[r:r + 1], v[r + 1:r + 2])
        hi = jnp.where(lane < SSD_HEAD_DIM, v[r + 2:r + 3], v[r + 3:r + 4])
        out.append(jnp.concatenate([lo, hi], axis=1))
    return out


def _ssd_kernel(xs_ref, bm_ref, cm_ref, z_ref, dt_ref, dtb_ref, alog_ref, dskip_ref, ng_ref, tri_ref, exp_ref,
                y_ref, hb_all, hf_scr, hb_scr, *, cpb):
    phase = pl.program_id(2)
    i = pl.program_id(3)
    nb = pl.num_programs(3)
    q = SSD_CHUNK
    gw = GROUP_WIDTH

    @pl.when((phase == 0) & (i == 0))
    def _():
        hb_scr[...] = jnp.zeros_like(hb_scr)

    @pl.when((phase == 1) & (i == 0))
    def _():
        hf_scr[...] = jnp.zeros_like(hf_scr)

    a = -jnp.exp(alog_ref[0])
    dt_bias = dtb_ref[0]

    def state_update(bm, xs, wexp, cd_row, h_ref):
        bmt = jnp.transpose(bm.astype(F32)).astype(BF16)
        st = jnp.dot(bmt, (xs * wexp).astype(BF16), preferred_element_type=F32)
        h_ref[...] = h_ref[...] * cd_row + st

    @pl.when(phase == 0)
    def _():
        def body(k, carry):
            ci = cpb - 1 - k
            r0 = pl.multiple_of(ci * q, q)
            dt, cs, ecs, w, cd = _dt_rows(dt_ref[ci], dt_bias, a, tri_ref[...])
            cols = _token_columns(cs, ecs, w).astype(BF16)
            wexp = jnp.dot(cols, exp_ref[:, 3 * gw:4 * gw], preferred_element_type=F32)
            hb_all[(nb - 1 - i) * cpb + ci] = hb_scr[...].astype(BF16)
            xs = xs_ref[pl.ds(r0, q), :].astype(F32)
            state_update(bm_ref[pl.ds(r0, q), :], xs, wexp, _head_row(cd)[1], hb_scr)
            return carry
        lax.fori_loop(0, cpb, body, 0)

    @pl.when(phase == 1)
    def _():
        t_idx = lax.broadcasted_iota(jnp.int32, (q, q), 0)
        s_idx = lax.broadcasted_iota(jnp.int32, (q, q), 1)
        lower = s_idx < t_idx
        diag = s_idx == t_idx
        lane_head = lax.broadcasted_iota(jnp.int32, (q, gw), 1) // SSD_HEAD_DIM
        nt_dims = (((1,), (1,)), ((), ()))

        def body(ci, carry):
            r0 = pl.multiple_of(ci * q, q)
            dt, cs, ecs, w, cd = _dt_rows(dt_ref[ci], dt_bias, a, tri_ref[...])
            cols = _token_columns(cs, ecs, w)
            expd = jnp.dot(cols.astype(BF16), exp_ref[...], preferred_element_type=F32)
            xs_b = xs_ref[pl.ds(r0, q), :]
            bm = bm_ref[pl.ds(r0, q), :]
            cm = cm_ref[pl.ds(r0, q), :]
            xs = xs_b.astype(F32)

            scores = lax.dot_general(cm, bm, nt_dims, preferred_element_type=F32)
            y = jnp.zeros((q, gw), F32)
            for r in range(HEADS_PER_GROUP):
                rb = HEADS_PER_GROUP + r
                colf = jnp.broadcast_to(cols[:, COL_CS + r:COL_CS + r + 1], (q, q))
                colb = jnp.broadcast_to(cols[:, COL_CS + rb:COL_CS + rb + 1], (q, q))
                arg = jnp.where(lower, colf - cs[r:r + 1], colb - cs[rb:rb + 1])
                dsel = jnp.where(lower, dt[r:r + 1],
                                 jnp.where(diag, dt[r:r + 1] + dt[rb:rb + 1], dt[rb:rb + 1]))
                m = (scores * jnp.exp(arg) * dsel).astype(BF16)
                x_r = jnp.where(lane_head == r, xs_b, jnp.zeros_like(xs_b))
                y = y + jnp.dot(m, x_r, preferred_element_type=F32)

            hcat = jnp.concatenate([hf_scr[...].astype(BF16), hb_all[i * cpb + ci]], axis=1)
            yoff = jnp.dot(cm, hcat, preferred_element_type=F32)
            y = y + yoff[:, 0:gw] * expd[:, 0:gw] + yoff[:, gw:2 * gw] * expd[:, gw:2 * gw]
            y = y + dskip_ref[...] * xs
            y = y * z_ref[pl.ds(r0, q), :].astype(F32)
            y = y * lax.rsqrt(jnp.mean(y * y, axis=-1, keepdims=True) + EPS) * ng_ref[...]
            y_ref[pl.ds(r0, q), :] = y.astype(BF16)

            state_update(bm, xs, expd[:, 2 * gw:3 * gw], _head_row(cd)[0], hf_scr)
            return carry
        lax.fori_loop(0, cpb, body, 0)


def _ssd_constants():
    q = SSD_CHUNK
    u = jnp.arange(q)[:, None]
    s = jnp.arange(q)[None, :]
    tri = jnp.concatenate([u <= s, u >= s, u > s, u < s, jnp.ones((q, q), bool)], axis=1).astype(BF16)
    row = jnp.arange(q)[:, None]
    col = jnp.arange(EXPAND_BLOCKS * GROUP_WIDTH)[None, :]
    src = jnp.array([COL_ECS, COL_ECS + HEADS_PER_GROUP, COL_W, COL_W + HEADS_PER_GROUP])[col // GROUP_WIDTH]
    expand = (row == src + (col % GROUP_WIDTH) // SSD_HEAD_DIM).astype(BF16)
    return tri, expand


def _ssd(proj, dt4, dtb, alog, dskip, ng, *, batch, seq_len, tq=1024):
    t = proj.shape[0]
    assert seq_len % tq == 0 and tq % SSD_CHUNK == 0
    nb = seq_len // tq
    cpb = tq // SSD_CHUNK
    tri, expand = _ssd_constants()
    gw = GROUP_WIDTH
    xs0, bm0, cm0 = BLK_XS * COL_BLOCK // gw, BLK_BM * COL_BLOCK // SSD_STATE, BLK_CM * COL_BLOCK // SSD_STATE

    def swept(b, p, i):
        return b * nb + jnp.where(p == 0, nb - 1 - i, i)

    def fwd_only(b, p, i):
        return b * nb + jnp.where(p == 0, 0, i)

    const2 = lambda b, g, p, i: (0, 0)
    return pl.pallas_call(
        functools.partial(_ssd_kernel, cpb=cpb),
        grid=(batch, SSD_GROUPS, 2, nb),
        in_specs=[
            pl.BlockSpec((tq, gw), lambda b, g, p, i: (swept(b, p, i), xs0 + g)),
            pl.BlockSpec((tq, SSD_STATE), lambda b, g, p, i: (swept(b, p, i), bm0 + g)),
            pl.BlockSpec((tq, SSD_STATE), lambda b, g, p, i: (fwd_only(b, p, i), cm0 + g)),
            pl.BlockSpec((tq, gw), lambda b, g, p, i: (fwd_only(b, p, i), g)),
            pl.BlockSpec((cpb, DT_ROWS, SSD_CHUNK), lambda b, g, p, i: (swept(b, p, i), g, 0)),
            pl.BlockSpec((1, DT_ROWS, LANES), lambda b, g, p, i: (g, 0, 0)),
            pl.BlockSpec((1, DT_ROWS, LANES), lambda b, g, p, i: (g, 0, 0)),
            pl.BlockSpec((1, gw), lambda b, g, p, i: (0, g)),
            pl.BlockSpec((1, gw), lambda b, g, p, i: (0, g)),
            pl.BlockSpec((SSD_CHUNK, TRI_BLOCKS * SSD_CHUNK), const2),
            pl.BlockSpec((SSD_CHUNK, EXPAND_BLOCKS * gw), const2),
        ],
        out_specs=pl.BlockSpec((tq, gw), lambda b, g, p, i: (fwd_only(b, p, i), g)),
        out_shape=jax.ShapeDtypeStruct((t, SSD_WIDTH), BF16),
        scratch_shapes=[
            pltpu.VMEM((seq_len // SSD_CHUNK, SSD_STATE, gw), BF16),
            pltpu.VMEM((SSD_STATE, gw), F32),
            pltpu.VMEM((SSD_STATE, gw), F32),
        ],
        compiler_params=pltpu.CompilerParams(
            dimension_semantics=("parallel", "parallel", "arbitrary", "arbitrary"),
            vmem_limit_bytes=48 * 1024 * 1024),
        name="ssd",
    )(proj, proj, proj, proj, dt4, dtb, alog, dskip, ng, tri, expand)


def _memkv_kernel(mem_ref, g_ref, w_ref, kv_ref):
    m = _rms(mem_ref[0], g_ref[...]).astype(BF16)
    kv_ref[0] = jnp.dot(m, w_ref[...], preferred_element_type=F32).astype(BF16)


def _memkv(mem, g, w_kv):
    b = mem.shape[0]
    return pl.pallas_call(
        _memkv_kernel,
        grid=(b,),
        in_specs=[
            pl.BlockSpec((1, MEM_LEN, D_MODEL), lambda i: (i, 0, 0)),
            pl.BlockSpec((1, D_MODEL), lambda i: (0, 0)),
            pl.BlockSpec((D_MODEL, 2 * D_MODEL), lambda i: (0, 0)),
        ],
        out_specs=pl.BlockSpec((1, MEM_LEN, 2 * D_MODEL), lambda i: (i, 0, 0)),
        out_shape=jax.ShapeDtypeStruct((b, MEM_LEN, 2 * D_MODEL), BF16),
        compiler_params=pltpu.CompilerParams(dimension_semantics=("parallel",)),
        name="memkv",
    )(mem, g, w_kv)


def _tail_kernel(yssd_ref, gate_ref, u_ref, v_ref, xq_ref, xgate_ref, m0_ref, m1_ref, m2_ref, x_ref,
                 k_ref, vmem_ref, ws_ref, bexp_ref, wbs_ref, wbg_ref, wbx_ref, wout_ref, gpost_ref,
                 out_ref, sv_scr, o_scr, *, tq):
    n_chunks = tq // GMLP_CHUNK
    gc = GMLP_CHUNK
    gwid = GMLP_GROUP_WIDTH

    for g in range(GMLP_GROUPS):
        cols = slice(g * gwid, (g + 1) * gwid)
        vcat = jnp.concatenate([v_ref[c * gc:(c + 1) * gc, cols] for c in range(n_chunks)], axis=1)
        sv = jnp.dot(ws_ref[g], vcat, preferred_element_type=F32)
        for c in range(n_chunks):
            sv_scr[c * gc:(c + 1) * gc, cols] = sv[:, c * gc:(c + 1) * gc]
    for c in range(n_chunks):
        rows = slice(c * gc, (c + 1) * gc)
        sv_scr[rows, :] = (u_ref[rows, :].astype(F32) * (sv_scr[rows, :] + bexp_ref[...])
                           * gate_ref[rows, :].astype(F32))

    nt_dims = (((1,), (1,)), ((), ()))
    scale = XATTN_HEAD_DIM ** -0.5
    for h in range(XATTN_HEADS):
        hc = slice(h * XATTN_HEAD_DIM, (h + 1) * XATTN_HEAD_DIM)
        s = lax.dot_general(xq_ref[:, hc], k_ref[0, :, hc], nt_dims, preferred_element_type=F32) * scale
        e = jnp.exp(s - jnp.max(s, axis=-1, keepdims=True))
        p = e / jnp.sum(e, axis=-1, keepdims=True)
        o = jnp.dot(p.astype(BF16), vmem_ref[0, :, hc], preferred_element_type=F32)
        o_scr[:, hc] = o * xgate_ref[:, hc].astype(F32)

    merged = m0_ref[...].astype(F32) * jnp.dot(yssd_ref[...], wbs_ref[...], preferred_element_type=F32)
    merged = merged + m1_ref[...].astype(F32) * jnp.dot(sv_scr[...].astype(BF16), wbg_ref[...],
                                                        preferred_element_type=F32)
    merged = merged + m2_ref[...].astype(F32) * jnp.dot(o_scr[...].astype(BF16), wbx_ref[...],
                                                        preferred_element_type=F32)
    out = jnp.dot(merged.astype(BF16), wout_ref[...], preferred_element_type=F32)
    out_ref[...] = x_ref[...] + _rms(out, gpost_ref[...])


def _tail(y_ssd, proj, x2, kv, ws, bexp, wbs, wbg, wbx, wout, gpost, *, seq_len, tq=256):
    t = x2.shape[0]
    assert seq_len % tq == 0 and tq % GMLP_CHUNK == 0
    per_seq = seq_len // tq
    resident = functools.partial(pl.BlockSpec, pipeline_mode=pl.Buffered(1))

    def proj_block(blk):
        return pl.BlockSpec((tq, COL_BLOCK), lambda i: (i, blk))

    return pl.pallas_call(
        functools.partial(_tail_kernel, tq=tq),
        grid=(t // tq,),
        in_specs=[
            pl.BlockSpec((tq, SSD_WIDTH), lambda i: (i, 0)),
            proj_block(BLK_GATE), proj_block(BLK_U), proj_block(BLK_V), proj_block(BLK_XQ), proj_block(BLK_XGATE),
            proj_block(BLK_MERGE), proj_block(BLK_MERGE + 1), proj_block(BLK_MERGE + 2),
            pl.BlockSpec((tq, D_MODEL), lambda i: (i, 0)),
            pl.BlockSpec((1, MEM_LEN, D_MODEL), lambda i: (i // per_seq, 0, 0)),
            pl.BlockSpec((1, MEM_LEN, D_MODEL), lambda i: (i // per_seq, 0, 1)),
            resident((GMLP_GROUPS, GMLP_CHUNK, GMLP_CHUNK), lambda i: (0, 0, 0)),
            resident((GMLP_CHUNK, GMLP_WIDTH), lambda i: (0, 0)),
            resident((SSD_WIDTH, D_MODEL), lambda i: (0, 0)),
            resident((GMLP_WIDTH, D_MODEL), lambda i: (0, 0)),
            resident((D_MODEL, D_MODEL), lambda i: (0, 0)),
            resident((D_MODEL, D_MODEL), lambda i: (0, 0)),
            resident((1, D_MODEL), lambda i: (0, 0)),
        ],
        out_specs=pl.BlockSpec((tq, D_MODEL), lambda i: (i, 0)),
        out_shape=jax.ShapeDtypeStruct((t, D_MODEL), F32),
        scratch_shapes=[
            pltpu.VMEM((tq, GMLP_WIDTH), F32),
            pltpu.VMEM((tq, D_MODEL), F32),
        ],
        compiler_params=pltpu.CompilerParams(
            dimension_semantics=("parallel",),
            vmem_limit_bytes=48 * 1024 * 1024),
        name="tail",
    )(y_ssd, proj, proj, proj, proj, proj, proj, proj, proj, x2, kv, kv,
      ws, bexp, wbs, wbg, wbx, wout, gpost)


def _split_bf16(w):
    hi = w.astype(BF16)
    return hi, (w - hi.astype(F32)).astype(BF16)


def _prep_layer(p, l):
    w_in = p['w_in'][l]
    dt0 = SSD_WIDTH + SSD_WIDTH + 2 * SSD_GROUPS * SSD_STATE
    dt1 = dt0 + 2 * SSD_HEADS
    w_main = jnp.concatenate([w_in[:, :dt0], w_in[:, dt1:]], axis=1).astype(BF16)
    w_dt = w_in[:, dt0:dt1].reshape(D_MODEL, 2, SSD_GROUPS, HEADS_PER_GROUP)
    w_dt = w_dt.transpose(2, 1, 3, 0).reshape(2 * SSD_HEADS, D_MODEL)
    wdt_hi, wdt_lo = _split_bf16(w_dt)

    def per_group_rows(v):
        v = v.reshape(2, SSD_GROUPS, HEADS_PER_GROUP).transpose(1, 0, 2).reshape(SSD_GROUPS, DT_ROWS, 1)
        return jnp.broadcast_to(v, (SSD_GROUPS, DT_ROWS, LANES))

    return dict(
        dtb=per_group_rows(p['dt_bias'][l]), alog=per_group_rows(p['a_log'][l]),
        dskip=jnp.repeat(p['d_skip'][l], SSD_HEAD_DIM).reshape(1, SSD_WIDTH),
        ssd_ng=p['ssd_norm_g'][l].reshape(1, SSD_WIDTH),
        gpre=p['norm_pre_g'][l].reshape(1, D_MODEL),
        w_main=w_main, wdt_hi=wdt_hi, wdt_lo=wdt_lo,
        convw=p['conv_w'][l], convb=p['conv_b'][l].reshape(1, -1),
        lng=p['gmlp_ln_g'][l].reshape(1, -1), lnb=p['gmlp_ln_b'][l].reshape(1, -1),
        ws=p['w_spatial'][l].astype(BF16),
        bexp=jnp.repeat(p['b_spatial'][l].T, GMLP_GROUP_WIDTH, axis=1),
        mem_g=p['mem_norm_g'][l].reshape(1, D_MODEL), w_kv=p['w_kv'][l].astype(BF16),
        wbs=p['w_br_ssd'][l].astype(BF16), wbg=p['w_br_gmlp'][l].astype(BF16),
        wbx=p['w_br_xattn'][l].astype(BF16), wout=p['w_out'][l].astype(BF16),
        gpost=p['norm_post_g'][l].reshape(1, D_MODEL),
    )


def _layer(x2, mem, prm, *, batch, seq_len):
    proj, dt4 = _inproj(x2, prm['gpre'], prm['w_main'], prm['wdt_hi'], prm['wdt_lo'],
                        prm['convw'], prm['convb'], prm['lng'], prm['lnb'], seq_len=seq_len)
    y_ssd = _ssd(proj, dt4, prm['dtb'], prm['alog'], prm['dskip'], prm['ssd_ng'], batch=batch, seq_len=seq_len)
    kv = _memkv(mem, prm['mem_g'], prm['w_kv'])
    return _tail(y_ssd, proj, x2, kv, prm['ws'], prm['bexp'], prm['wbs'], prm['wbg'], prm['wbx'], prm['wout'],
                 prm['gpost'], seq_len=seq_len)


def kernel(x, mem, norm_pre_g, w_in, conv_w, conv_b, dt_bias, a_log, d_skip, ssd_norm_g, gmlp_ln_g, gmlp_ln_b, w_spatial, b_spatial, mem_norm_g, w_kv, w_br_ssd, w_br_gmlp, w_br_xattn, w_out, norm_post_g):
    p = dict(norm_pre_g=norm_pre_g, w_in=w_in, conv_w=conv_w, conv_b=conv_b, dt_bias=dt_bias, a_log=a_log,
             d_skip=d_skip, ssd_norm_g=ssd_norm_g, gmlp_ln_g=gmlp_ln_g, gmlp_ln_b=gmlp_ln_b, w_spatial=w_spatial,
             b_spatial=b_spatial, mem_norm_g=mem_norm_g, w_kv=w_kv, w_br_ssd=w_br_ssd, w_br_gmlp=w_br_gmlp,
             w_br_xattn=w_br_xattn, w_out=w_out, norm_post_g=norm_post_g)
    batch, seq_len, _ = x.shape
    x2 = x.reshape(batch * seq_len, D_MODEL)
    for l in range(w_in.shape[0]):
        x2 = _layer(x2, mem, _prep_layer(p, l), batch=batch, seq_len=seq_len)
    return x2.reshape(batch, seq_len, D_MODEL)
```

```python
import functools

import jax
import jax.numpy as jnp
from jax import lax
from jax.experimental import pallas as pl
from jax.experimental.pallas import tpu as pltpu

F32 = jnp.float32
BF16 = jnp.bfloat16

EPS = 1e-6
D_MODEL = 1024
N_BRANCH = 3

SSD_WIDTH = 2 * D_MODEL
SSD_HEAD_DIM = 64
SSD_HEADS = SSD_WIDTH // SSD_HEAD_DIM
SSD_GROUPS = 8
SSD_STATE = 128
SSD_CONV = 5
SSD_CHUNK = 128
HEADS_PER_GROUP = SSD_HEADS // SSD_GROUPS
GROUP_WIDTH = HEADS_PER_GROUP * SSD_HEAD_DIM
DT_ROWS = 2 * HEADS_PER_GROUP

GMLP_WIDTH = D_MODEL
GMLP_GROUPS = 8
GMLP_CHUNK = 128
GMLP_GROUP_WIDTH = GMLP_WIDTH // GMLP_GROUPS

XATTN_HEADS = 4
XATTN_HEAD_DIM = D_MODEL // XATTN_HEADS
MEM_LEN = 256

LANES = 128
BF16_ROWS = 16

COL_BLOCK = 1024
BLK_Z, BLK_XS, BLK_BM, BLK_CM, BLK_GATE, BLK_U, BLK_V, BLK_XQ, BLK_XGATE, BLK_MERGE = 0, 2, 4, 5, 6, 7, 8, 9, 10, 11
N_COL_BLOCKS = 14
PROJ_COLS = N_COL_BLOCKS * COL_BLOCK
HALO = BF16_ROWS
CONV_PAD = SSD_CONV // 2


def _silu(x):
    return x * jax.nn.sigmoid(x)


def _gelu_tanh(x):
    c = 0.7978845608028654
    return x * (0.5 * (1.0 + jnp.tanh(c * (x + 0.044715 * (x * x * x)))))


def _rms(x, g):
    return x * lax.rsqrt(jnp.mean(x * x, axis=-1, keepdims=True) + EPS) * g


def _inproj_kernel(xprev_ref, x_ref, xnext_ref, gpre_ref, w_ref, wdth_ref, wdtl_ref,
                   convw_ref, convb_ref, lng_ref, lnb_ref,
                   out_ref, dt_ref, h_scr, acc_scr, *, tm, sub, tiles_per_seq):
    i = pl.program_id(0)
    j = pl.program_id(1)
    n_sub = tm // sub
    nt_dims = (((1,), (1,)), ((), ()))

    @pl.when(j == 0)
    def _():
        g = gpre_ref[...]
        first = (i % tiles_per_seq) == 0
        last = (i % tiles_per_seq) == tiles_per_seq - 1
        hp = _rms(xprev_ref[...], g) * jnp.where(first, 0.0, 1.0)
        hn = _rms(xnext_ref[...], g) * jnp.where(last, 0.0, 1.0)
        h_scr[0:HALO, :] = hp.astype(BF16)
        h_scr[HALO + tm:HALO + tm + HALO, :] = hn.astype(BF16)
        for s in range(n_sub):
            h = _rms(x_ref[s * sub:(s + 1) * sub, :], g)
            hb = h.astype(BF16)
            h_scr[HALO + s * sub:HALO + (s + 1) * sub, :] = hb
            hl = (h - hb.astype(F32)).astype(BF16)
            dt = (lax.dot_general(wdth_ref[...], hb, nt_dims, preferred_element_type=F32)
                  + lax.dot_general(wdth_ref[...], hl, nt_dims, preferred_element_type=F32)
                  + lax.dot_general(wdtl_ref[...], hb, nt_dims, preferred_element_type=F32))
            for c in range(sub // SSD_CHUNK):
                dt_ref[s * (sub // SSD_CHUNK) + c] = dt[:, c * SSD_CHUNK:(c + 1) * SSD_CHUNK]

    def plain(act):
        for s in range(n_sub):
            rows = h_scr[HALO + s * sub:HALO + (s + 1) * sub, :]
            acc = jnp.dot(rows, w_ref[...], preferred_element_type=F32)
            out_ref[s * sub:(s + 1) * sub, :] = act(acc).astype(BF16)

    is_silu = (j == BLK_Z) | (j == BLK_Z + 1) | (j == BLK_GATE) | (j == BLK_XGATE)
    is_conv = (j >= BLK_XS) & (j <= BLK_CM)
    is_sig = j >= BLK_MERGE

    @pl.when(is_silu)
    def _():
        plain(_silu)

    @pl.when(j == BLK_XQ)
    def _():
        plain(lambda a: a)

    @pl.when(j == BLK_U)
    def _():
        plain(_gelu_tanh)

    @pl.when(j == BLK_V)
    def _():
        def gelu_ln(a):
            v = _gelu_tanh(a)
            mu = jnp.mean(v, axis=-1, keepdims=True)
            vc = v - mu
            return vc * lax.rsqrt(jnp.mean(vc * vc, axis=-1, keepdims=True) + EPS) * lng_ref[...] + lnb_ref[...]
        plain(gelu_ln)

    @pl.when(is_sig)
    def _():
        plain(jax.nn.sigmoid)

    @pl.when(is_conv)
    def _():
        for s in range(n_sub):
            rows = h_scr[s * sub:(s + 1) * sub + 2 * HALO, :]
            acc_scr[...] = jnp.dot(rows, w_ref[...], preferred_element_type=F32)
            y = convb_ref[...]
            for k in range(SSD_CONV):
                off = HALO - CONV_PAD + k
                y = y + convw_ref[k:k + 1, :] * acc_scr[off:off + sub, :]
            out_ref[s * sub:(s + 1) * sub, :] = _silu(y).astype(BF16)


def _inproj(x2, gpre, w_main, wdt_hi, wdt_lo, convw, convb, lng, lnb, *, seq_len, tm=1024, sub=512):
    t = x2.shape[0]
    assert t % tm == 0 and seq_len % tm == 0 and tm % sub == 0 and sub % SSD_CHUNK == 0
    n_tiles = t // tm
    halo_blocks = t // HALO
    per_tile = tm // HALO
    kernel = functools.partial(_inproj_kernel, tm=tm, sub=sub, tiles_per_seq=seq_len // tm)
    conv_idx = lambda i, j: (0, jnp.clip(j - BLK_XS, 0, BLK_CM - BLK_XS))
    return pl.pallas_call(
        kernel,
        grid=(n_tiles, N_COL_BLOCKS),
        in_specs=[
            pl.BlockSpec((HALO, D_MODEL), lambda i, j: (jnp.maximum(i * per_tile - 1, 0), 0)),
            pl.BlockSpec((tm, D_MODEL), lambda i, j: (i, 0)),
            pl.BlockSpec((HALO, D_MODEL), lambda i, j: (jnp.minimum((i + 1) * per_tile, halo_blocks - 1), 0)),
            pl.BlockSpec((1, D_MODEL), lambda i, j: (0, 0)),
            pl.BlockSpec((D_MODEL, COL_BLOCK), lambda i, j: (0, j)),
            pl.BlockSpec((2 * SSD_HEADS, D_MODEL), lambda i, j: (0, 0)),
            pl.BlockSpec((2 * SSD_HEADS, D_MODEL), lambda i, j: (0, 0)),
            pl.BlockSpec((SSD_CONV, COL_BLOCK), conv_idx),
            pl.BlockSpec((1, COL_BLOCK), conv_idx),
            pl.BlockSpec((1, COL_BLOCK), lambda i, j: (0, 0)),
            pl.BlockSpec((1, COL_BLOCK), lambda i, j: (0, 0)),
        ],
        out_specs=[
            pl.BlockSpec((tm, COL_BLOCK), lambda i, j: (i, j)),
            pl.BlockSpec((tm // SSD_CHUNK, 2 * SSD_HEADS, SSD_CHUNK), lambda i, j: (i, 0, 0)),
        ],
        out_shape=[
            jax.ShapeDtypeStruct((t, PROJ_COLS), BF16),
            jax.ShapeDtypeStruct((t // SSD_CHUNK, 2 * SSD_HEADS, SSD_CHUNK), F32),
        ],
        scratch_shapes=[
            pltpu.VMEM((tm + 2 * HALO, D_MODEL), BF16),
            pltpu.VMEM((sub + 2 * HALO, COL_BLOCK), F32),
        ],
        compiler_params=pltpu.CompilerParams(
            dimension_semantics=("parallel", "arbitrary"),
            vmem_limit_bytes=48 * 1024 * 1024),
        name="inproj",
    )(x2, x2, x2, gpre, w_main, wdt_hi, wdt_lo, convw, convb, lng, lnb)


TRI_BLOCKS = 5
EXPAND_BLOCKS = 4
COL_CS, COL_ECS, COL_W = 0, DT_ROWS, 2 * DT_ROWS


def _softplus(x):
    return jnp.maximum(x, 0.0) + jnp.log1p(jnp.exp(-jnp.abs(x)))


def _dt_rows(dt_raw, dt_bias, a, tri):
    rows, q = dt_raw.shape
    dt = _softplus(dt_raw + dt_bias)
    da = dt * a
    hi = da.astype(BF16).astype(F32)
    r1 = da - hi
    mid = r1.astype(BF16).astype(F32)
    lo = r1 - mid
    parts = [hi, mid, lo]
    if (3 * rows) % BF16_ROWS:
        parts.append(jnp.zeros((BF16_ROWS - (3 * rows) % BF16_ROWS, q), F32))
    sums = jnp.dot(jnp.concatenate(parts, axis=0).astype(BF16), tri, preferred_element_type=F32)
    s = sums[0:rows] + sums[rows:2 * rows] + sums[2 * rows:3 * rows]
    fwd = lax.broadcasted_iota(jnp.int32, (rows, q), 0) % DT_ROWS < HEADS_PER_GROUP
    cs = jnp.where(fwd, s[:, 0:q], s[:, q:2 * q])
    rem = jnp.where(fwd, s[:, 2 * q:3 * q], s[:, 3 * q:4 * q])
    total = s[:, 4 * q:5 * q]
    return dt, cs, jnp.exp(cs), dt * jnp.exp(rem), jnp.exp(total)


def _token_columns(cs, ecs, w):
    pad = jnp.zeros((SSD_CHUNK - 3 * DT_ROWS, SSD_CHUNK), F32)
    return jnp.transpose(jnp.concatenate([cs, ecs, w, pad], axis=0))


def _head_row(v, d):
    lane = lax.broadcasted_iota(jnp.int32, (1, LANES), 1)
    r = d * HEADS_PER_GROUP
    lo = jnp.where(lane < SSD_HEAD_DIM, v[r:r + 1], v[r + 1:r + 2])
    hi = jnp.where(lane < SSD_HEAD_DIM, v[r + 2:r + 3], v[r + 3:r + 4])
    return jnp.concatenate([lo, hi], axis=1)


def _ssd_kernel(xs_ref, bm_ref, cm_ref, z_ref, dt_ref, dtb_ref, alog_ref, dskip_ref, ng_ref, tri_ref, exp_ref,
                y_ref, hb_all, hf_scr, hb_scr, *, cpb):
    phase = pl.program_id(2)
    i = pl.program_id(3)
    nb = pl.num_programs(3)
    q = SSD_CHUNK
    gw = GROUP_WIDTH
    chunks = range(cpb)
    rows_of = lambda c: slice(c * q, (c + 1) * q)
    heads_of = lambda v, c: v[c * DT_ROWS:(c + 1) * DT_ROWS]

    @pl.when((phase == 0) & (i == 0))
    def _():
        hb_scr[...] = jnp.zeros_like(hb_scr)

    @pl.when((phase == 1) & (i == 0))
    def _():
        hf_scr[...] = jnp.zeros_like(hf_scr)

    def token_scalars():
        tile = lambda v: jnp.concatenate([v] * cpb, axis=0)
        dt_raw = dt_ref[...].reshape(cpb * DT_ROWS, q)
        dt, cs, ecs, w, cd = _dt_rows(dt_raw, tile(dtb_ref[0]), tile(-jnp.exp(alog_ref[0])), tri_ref[...])
        cols = [_token_columns(heads_of(cs, c), heads_of(ecs, c), heads_of(w, c)) for c in chunks]
        return dt, cs, cd, cols

    def local_state(c, wexp):
        bmt = jnp.transpose(bm_ref[rows_of(c), :].astype(F32)).astype(BF16)
        xw = (xs_ref[rows_of(c), :].astype(F32) * wexp).astype(BF16)
        return jnp.dot(bmt, xw, preferred_element_type=F32)

    @pl.when(phase == 0)
    def _():
        dt, cs, cd, cols = token_scalars()
        wexp = jnp.dot(jnp.concatenate(cols, axis=0).astype(BF16), exp_ref[:, 3 * gw:4 * gw],
                       preferred_element_type=F32)
        st = [local_state(c, wexp[rows_of(c)]) for c in chunks]
        h = hb_scr[...]
        for c in reversed(chunks):
            hb_all[(nb - 1 - i) * cpb + c] = h.astype(BF16)
            h = h * _head_row(heads_of(cd, c), 1) + st[c]
        hb_scr[...] = h

    @pl.when(phase == 1)
    def _():
        t_idx = lax.broadcasted_iota(jnp.int32, (q, q), 0)
        s_idx = lax.broadcasted_iota(jnp.int32, (q, q), 1)
        lower = s_idx < t_idx
        diag = s_idx == t_idx
        lane_head = lax.broadcasted_iota(jnp.int32, (q, gw), 1) // SSD_HEAD_DIM
        nt_dims = (((1,), (1,)), ((), ()))

        dt, cs, cd, cols = token_scalars()
        expd = jnp.dot(jnp.concatenate(cols, axis=0).astype(BF16), exp_ref[...],
                       preferred_element_type=F32)
        scores = [lax.dot_general(cm_ref[rows_of(c), :], bm_ref[rows_of(c), :], nt_dims,
                                  preferred_element_type=F32) for c in chunks]
        st = [local_state(c, expd[rows_of(c), 2 * gw:3 * gw]) for c in chunks]

        h = hf_scr[...]
        yoff = []
        for c in chunks:
            hcat = jnp.concatenate([h.astype(BF16), hb_all[i * cpb + c]], axis=1)
            yoff.append(jnp.dot(cm_ref[rows_of(c), :], hcat, preferred_element_type=F32))
            h = h * _head_row(heads_of(cd, c), 0) + st[c]
        hf_scr[...] = h

        for c in chunks:
            dt_c, cs_c = heads_of(dt, c), heads_of(cs, c)
            xs_b = xs_ref[rows_of(c), :]
            y = jnp.zeros((q, gw), F32)
            for r in range(HEADS_PER_GROUP):
                rb = HEADS_PER_GROUP + r
                colf = jnp.broadcast_to(cols[c][:, COL_CS + r:COL_CS + r + 1], (q, q))
                colb = jnp.broadcast_to(cols[c][:, COL_CS + rb:COL_CS + rb + 1], (q, q))
                arg = jnp.where(lower, colf - cs_c[r:r + 1], colb - cs_c[rb:rb + 1])
                dsel = jnp.where(lower, dt_c[r:r + 1],
                                 jnp.where(diag, dt_c[r:r + 1] + dt_c[rb:rb + 1], dt_c[rb:rb + 1]))
                m = (scores[c] * jnp.exp(arg) * dsel).astype(BF16)
                x_r = jnp.where(lane_head == r, xs_b, jnp.zeros_like(xs_b))
                y = y + jnp.dot(m, x_r, preferred_element_type=F32)

            ex = expd[rows_of(c)]
            y = y + yoff[c][:, 0:gw] * ex[:, 0:gw] + yoff[c][:, gw:2 * gw] * ex[:, gw:2 * gw]
            y = y + dskip_ref[...] * xs_b.astype(F32)
            y = y * z_ref[rows_of(c), :].astype(F32)
            y = y * lax.rsqrt(jnp.mean(y * y, axis=-1, keepdims=True) + EPS) * ng_ref[...]
            y_ref[rows_of(c), :] = y.astype(BF16)


def _ssd_constants():
    q = SSD_CHUNK
    u = jnp.arange(q)[:, None]
    s = jnp.arange(q)[None, :]
    tri = jnp.concatenate([u <= s, u >= s, u > s, u < s, jnp.ones((q, q), bool)], axis=1).astype(BF16)
    row = jnp.arange(q)[:, None]
    col = jnp.arange(EXPAND_BLOCKS * GROUP_WIDTH)[None, :]
    src = jnp.array([COL_ECS, COL_ECS + HEADS_PER_GROUP, COL_W, COL_W + HEADS_PER_GROUP])[col // GROUP_WIDTH]
    expand = (row == src + (col % GROUP_WIDTH) // SSD_HEAD_DIM).astype(BF16)
    return tri, expand


def _ssd(proj, dt4, dtb, alog, dskip, ng, *, batch, seq_len, tq=1024):
    t = proj.shape[0]
    assert seq_len % tq == 0 and tq % SSD_CHUNK == 0
    nb = seq_len // tq
    cpb = tq // SSD_CHUNK
    tri, expand = _ssd_constants()
    gw = GROUP_WIDTH
    xs0, bm0, cm0 = BLK_XS * COL_BLOCK // gw, BLK_BM * COL_BLOCK // SSD_STATE, BLK_CM * COL_BLOCK // SSD_STATE

    def swept(b, p, i):
        return b * nb + jnp.where(p == 0, nb - 1 - i, i)

    def fwd_only(b, p, i):
        return b * nb + jnp.where(p == 0, 0, i)

    const2 = lambda b, g, p, i: (0, 0)
    return pl.pallas_call(
        functools.partial(_ssd_kernel, cpb=cpb),
        grid=(batch, SSD_GROUPS, 2, nb),
        in_specs=[
            pl.BlockSpec((tq, gw), lambda b, g, p, i: (swept(b, p, i), xs0 + g)),
            pl.BlockSpec((tq, SSD_STATE), lambda b, g, p, i: (swept(b, p, i), bm0 + g)),
            pl.BlockSpec((tq, SSD_STATE), lambda b, g, p, i: (fwd_only(b, p, i), cm0 + g)),
            pl.BlockSpec((tq, gw), lambda b, g, p, i: (fwd_only(b, p, i), g)),
            pl.BlockSpec((cpb, DT_ROWS, SSD_CHUNK), lambda b, g, p, i: (swept(b, p, i), g, 0)),
            pl.BlockSpec((1, DT_ROWS, LANES), lambda b, g, p, i: (g, 0, 0)),
            pl.BlockSpec((1, DT_ROWS, LANES), lambda b, g, p, i: (g, 0, 0)),
            pl.BlockSpec((1, gw), lambda b, g, p, i: (0, g)),
            pl.BlockSpec((1, gw), lambda b, g, p, i: (0, g)),
            pl.BlockSpec((SSD_CHUNK, TRI_BLOCKS * SSD_CHUNK), const2),
            pl.BlockSpec((SSD_CHUNK, EXPAND_BLOCKS * gw), const2),
        ],
        out_specs=pl.BlockSpec((tq, gw), lambda b, g, p, i: (fwd_only(b, p, i), g)),
        out_shape=jax.ShapeDtypeStruct((t, SSD_WIDTH), BF16),
        scratch_shapes=[
            pltpu.VMEM((seq_len // SSD_CHUNK, SSD_STATE, gw), BF16),
            pltpu.VMEM((SSD_STATE, gw), F32),
            pltpu.VMEM((SSD_STATE, gw), F32),
        ],
        compiler_params=pltpu.CompilerParams(
            dimension_semantics=("parallel", "parallel", "arbitrary", "arbitrary"),
            vmem_limit_bytes=48 * 1024 * 1024),
        name="ssd",
    )(proj, proj, proj, proj, dt4, dtb, alog, dskip, ng, tri, expand)


def _memkv_kernel(mem_ref, g_ref, w_ref, kv_ref):
    m = _rms(mem_ref[0], g_ref[...]).astype(BF16)
    kv_ref[0] = jnp.dot(m, w_ref[...], preferred_element_type=F32).astype(BF16)


def _memkv(mem, g, w_kv):
    b = mem.shape[0]
    return pl.pallas_call(
        _memkv_kernel,
        grid=(b,),
        in_specs=[
            pl.BlockSpec((1, MEM_LEN, D_MODEL), lambda i: (i, 0, 0)),
            pl.BlockSpec((1, D_MODEL), lambda i: (0, 0)),
            pl.BlockSpec((D_MODEL, 2 * D_MODEL), lambda i: (0, 0)),
        ],
        out_specs=pl.BlockSpec((1, MEM_LEN, 2 * D_MODEL), lambda i: (i, 0, 0)),
        out_shape=jax.ShapeDtypeStruct((b, MEM_LEN, 2 * D_MODEL), BF16),
        compiler_params=pltpu.CompilerParams(dimension_semantics=("parallel",)),
        name="memkv",
    )(mem, g, w_kv)


def _tail_kernel(yssd_ref, gate_ref, u_ref, v_ref, xq_ref, xgate_ref, m0_ref, m1_ref, m2_ref, x_ref,
                 k_ref, vmem_ref, ws_ref, bexp_ref, wbs_ref, wbg_ref, wbx_ref, wout_ref, gpost_ref,
                 out_ref, sv_scr, o_scr, *, tq):
    n_chunks = tq // GMLP_CHUNK
    gc = GMLP_CHUNK
    gwid = GMLP_GROUP_WIDTH

    for g in range(GMLP_GROUPS):
        cols = slice(g * gwid, (g + 1) * gwid)
        vcat = jnp.concatenate([v_ref[c * gc:(c + 1) * gc, cols] for c in range(n_chunks)], axis=1)
        sv = jnp.dot(ws_ref[g], vcat, preferred_element_type=F32)
        for c in range(n_chunks):
            sv_scr[c * gc:(c + 1) * gc, cols] = sv[:, c * gc:(c + 1) * gc]
    for c in range(n_chunks):
        rows = slice(c * gc, (c + 1) * gc)
        sv_scr[rows, :] = (u_ref[rows, :].astype(F32) * (sv_scr[rows, :] + bexp_ref[...])
                           * gate_ref[rows, :].astype(F32))

    nt_dims = (((1,), (1,)), ((), ()))
    scale = XATTN_HEAD_DIM ** -0.5
    for h in range(XATTN_HEADS):
        hc = slice(h * XATTN_HEAD_DIM, (h + 1) * XATTN_HEAD_DIM)
        s = lax.dot_general(xq_ref[:, hc], k_ref[0, :, hc], nt_dims, preferred_element_type=F32) * scale
        e = jnp.exp(s - jnp.max(s, axis=-1, keepdims=True))
        p = e / jnp.sum(e, axis=-1, keepdims=True)
        o = jnp.dot(p.astype(BF16), vmem_ref[0, :, hc], preferred_element_type=F32)
        o_scr[:, hc] = o * xgate_ref[:, hc].astype(F32)

    merged = m0_ref[...].astype(F32) * jnp.dot(yssd_ref[...], wbs_ref[...], preferred_element_type=F32)
    merged = merged + m1_ref[...].astype(F32) * jnp.dot(sv_scr[...].astype(BF16), wbg_ref[...],
                                                        preferred_element_type=F32)
    merged = merged + m2_ref[...].astype(F32) * jnp.dot(o_scr[...].astype(BF16), wbx_ref[...],
                                                        preferred_element_type=F32)
    out = jnp.dot(merged.astype(BF16), wout_ref[...], preferred_element_type=F32)
    out_ref[...] = x_ref[...] + _rms(out, gpost_ref[...])


def _tail(y_ssd, proj, x2, kv, ws, bexp, wbs, wbg, wbx, wout, gpost, *, seq_len, tq=256):
    t = x2.shape[0]
    assert seq_len % tq == 0 and tq % GMLP_CHUNK == 0
    per_seq = seq_len // tq
    resident = functools.partial(pl.BlockSpec, pipeline_mode=pl.Buffered(1))

    def proj_block(blk):
        return pl.BlockSpec((tq, COL_BLOCK), lambda i: (i, blk))

    return pl.pallas_call(
        functools.partial(_tail_kernel, tq=tq),
        grid=(t // tq,),
        in_specs=[
            pl.BlockSpec((tq, SSD_WIDTH), lambda i: (i, 0)),
            proj_block(BLK_GATE), proj_block(BLK_U), proj_block(BLK_V), proj_block(BLK_XQ), proj_block(BLK_XGATE),
            proj_block(BLK_MERGE), proj_block(BLK_MERGE + 1), proj_block(BLK_MERGE + 2),
            pl.BlockSpec((tq, D_MODEL), lambda i: (i, 0)),
            pl.BlockSpec((1, MEM_LEN, D_MODEL), lambda i: (i // per_seq, 0, 0)),
            pl.BlockSpec((1, MEM_LEN, D_MODEL), lambda i: (i // per_seq, 0, 1)),
            resident((GMLP_GROUPS, GMLP_CHUNK, GMLP_CHUNK), lambda i: (0, 0, 0)),
            resident((GMLP_CHUNK, GMLP_WIDTH), lambda i: (0, 0)),
            resident((SSD_WIDTH, D_MODEL), lambda i: (0, 0)),
            resident((GMLP_WIDTH, D_MODEL), lambda i: (0, 0)),
            resident((D_MODEL, D_MODEL), lambda i: (0, 0)),
            resident((D_MODEL, D_MODEL), lambda i: (0, 0)),
            resident((1, D_MODEL), lambda i: (0, 0)),
        ],
        out_specs=pl.BlockSpec((tq, D_MODEL), lambda i: (i, 0)),
        out_shape=jax.ShapeDtypeStruct((t, D_MODEL), F32),
        scratch_shapes=[
            pltpu.VMEM((tq, GMLP_WIDTH), F32),
            pltpu.VMEM((tq, D_MODEL), F32),
        ],
        compiler_params=pltpu.CompilerParams(
            dimension_semantics=("parallel",),
            vmem_limit_bytes=48 * 1024 * 1024),
        name="tail",
    )(y_ssd, proj, proj, proj, proj, proj, proj, proj, proj, x2, kv, kv,
      ws, bexp, wbs, wbg, wbx, wout, gpost)


def _split_bf16(w):
    hi = w.astype(BF16)
    return hi, (w - hi.astype(F32)).astype(BF16)


def _prep_layer(p, l):
    w_in = p['w_in'][l]
    dt0 = SSD_WIDTH + SSD_WIDTH + 2 * SSD_GROUPS * SSD_STATE
    dt1 = dt0 + 2 * SSD_HEADS
    w_main = jnp.concatenate([w_in[:, :dt0], w_in[:, dt1:]], axis=1).astype(BF16)
    w_dt = w_in[:, dt0:dt1].reshape(D_MODEL, 2, SSD_GROUPS, HEADS_PER_GROUP)
    w_dt = w_dt.transpose(2, 1, 3, 0).reshape(2 * SSD_HEADS, D_MODEL)
    wdt_hi, wdt_lo = _split_bf16(w_dt)

    def per_group_rows(v):
        v = v.reshape(2, SSD_GROUPS, HEADS_PER_GROUP).transpose(1, 0, 2).reshape(SSD_GROUPS, DT_ROWS, 1)
        return jnp.broadcast_to(v, (SSD_GROUPS, DT_ROWS, LANES))

    return dict(
        dtb=per_group_rows(p['dt_bias'][l]), alog=per_group_rows(p['a_log'][l]),
        dskip=jnp.repeat(p['d_skip'][l], SSD_HEAD_DIM).reshape(1, SSD_WIDTH),
        ssd_ng=p['ssd_norm_g'][l].reshape(1, SSD_WIDTH),
        gpre=p['norm_pre_g'][l].reshape(1, D_MODEL),
        w_main=w_main, wdt_hi=wdt_hi, wdt_lo=wdt_lo,
        convw=p['conv_w'][l], convb=p['conv_b'][l].reshape(1, -1),
        lng=p['gmlp_ln_g'][l].reshape(1, -1), lnb=p['gmlp_ln_b'][l].reshape(1, -1),
        ws=p['w_spatial'][l].astype(BF16),
        bexp=jnp.repeat(p['b_spatial'][l].T, GMLP_GROUP_WIDTH, axis=1),
        mem_g=p['mem_norm_g'][l].reshape(1, D_MODEL), w_kv=p['w_kv'][l].astype(BF16),
        wbs=p['w_br_ssd'][l].astype(BF16), wbg=p['w_br_gmlp'][l].astype(BF16),
        wbx=p['w_br_xattn'][l].astype(BF16), wout=p['w_out'][l].astype(BF16),
        gpost=p['norm_post_g'][l].reshape(1, D_MODEL),
    )


def _layer(x2, mem, prm, *, batch, seq_len):
    proj, dt4 = _inproj(x2, prm['gpre'], prm['w_main'], prm['wdt_hi'], prm['wdt_lo'],
                        prm['convw'], prm['convb'], prm['lng'], prm['lnb'], seq_len=seq_len)
    y_ssd = _ssd(proj, dt4, prm['dtb'], prm['alog'], prm['dskip'], prm['ssd_ng'], batch=batch, seq_len=seq_len)
    kv = _memkv(mem, prm['mem_g'], prm['w_kv'])
    return _tail(y_ssd, proj, x2, kv, prm['ws'], prm['bexp'], prm['wbs'], prm['wbg'], prm['wbx'], prm['wout'],
                 prm['gpost'], seq_len=seq_len)


def kernel(x, mem, norm_pre_g, w_in, conv_w, conv_b, dt_bias, a_log, d_skip, ssd_norm_g, gmlp_ln_g, gmlp_ln_b, w_spatial, b_spatial, mem_norm_g, w_kv, w_br_ssd, w_br_gmlp, w_br_xattn, w_out, norm_post_g):
    p = dict(norm_pre_g=norm_pre_g, w_in=w_in, conv_w=conv_w, conv_b=conv_b, dt_bias=dt_bias, a_log=a_log,
             d_skip=d_skip, ssd_norm_g=ssd_norm_g, gmlp_ln_g=gmlp_ln_g, gmlp_ln_b=gmlp_ln_b, w_spatial=w_spatial,
             b_spatial=b_spatial, mem_norm_g=mem_norm_g, w_kv=w_kv, w_br_ssd=w_br_ssd, w_br_gmlp=w_br_gmlp,
             w_br_xattn=w_br_xattn, w_out=w_out, norm_post_g=norm_post_g)
    batch, seq_len, _ = x.shape
    x2 = x.reshape(batch * seq_len, D_MODEL)
    for l in range(w_in.shape[0]):
        x2 = _layer(x2, mem, _prep_layer(p, l), batch=batch, seq_len=seq_len)
    return x2.reshape(batch, seq_len, D_MODEL)
```

```python
import functools

import jax
import jax.numpy as jnp
from jax import lax
from jax.experimental import pallas as pl
from jax.experimental.pallas import tpu as pltpu

F32 = jnp.float32
BF16 = jnp.bfloat16

EPS = 1e-6
D_MODEL = 1024
N_BRANCH = 3

SSD_WIDTH = 2 * D_MODEL
SSD_HEAD_DIM = 64
SSD_HEADS = SSD_WIDTH // SSD_HEAD_DIM
SSD_GROUPS = 8
SSD_STATE = 128
SSD_CONV = 5
SSD_CHUNK = 128
HEADS_PER_GROUP = SSD_HEADS // SSD_GROUPS
GROUP_WIDTH = HEADS_PER_GROUP * SSD_HEAD_DIM
DT_ROWS = 2 * HEADS_PER_GROUP

GMLP_WIDTH = D_MODEL
GMLP_GROUPS = 8
GMLP_CHUNK = 128
GMLP_GROUP_WIDTH = GMLP_WIDTH // GMLP_GROUPS

XATTN_HEADS = 4
XATTN_HEAD_DIM = D_MODEL // XATTN_HEADS
MEM_LEN = 256

LANES = 128
BF16_ROWS = 16

COL_BLOCK = 1024
BLK_Z, BLK_XS, BLK_BM, BLK_CM, BLK_GATE, BLK_U, BLK_V, BLK_XQ, BLK_XGATE, BLK_MERGE = 0, 2, 4, 5, 6, 7, 8, 9, 10, 11
N_COL_BLOCKS = 14
PROJ_COLS = N_COL_BLOCKS * COL_BLOCK
HALO = BF16_ROWS
CONV_PAD = SSD_CONV // 2


def _silu(x):
    return x * jax.nn.sigmoid(x)


def _gelu_tanh(x):
    c = 0.7978845608028654
    return x * (0.5 * (1.0 + jnp.tanh(c * (x + 0.044715 * (x * x * x)))))


def _rms(x, g):
    return x * lax.rsqrt(jnp.mean(x * x, axis=-1, keepdims=True) + EPS) * g


def _inproj_kernel(xprev_ref, x_ref, xnext_ref, gpre_ref, w_ref, wdth_ref, wdtl_ref,
                   convw_ref, convb_ref, lng_ref, lnb_ref,
                   out_ref, dt_ref, h_scr, acc_scr, *, tm, sub, tiles_per_seq):
    i = pl.program_id(0)
    j = pl.program_id(1)
    n_sub = tm // sub
    nt_dims = (((1,), (1,)), ((), ()))

    @pl.when(j == 0)
    def _():
        g = gpre_ref[...]
        first = (i % tiles_per_seq) == 0
        last = (i % tiles_per_seq) == tiles_per_seq - 1
        hp = _rms(xprev_ref[...], g) * jnp.where(first, 0.0, 1.0)
        hn = _rms(xnext_ref[...], g) * jnp.where(last, 0.0, 1.0)
        h_scr[0:HALO, :] = hp.astype(BF16)
        h_scr[HALO + tm:HALO + tm + HALO, :] = hn.astype(BF16)
        for s in range(n_sub):
            h = _rms(x_ref[s * sub:(s + 1) * sub, :], g)
            hb = h.astype(BF16)
            h_scr[HALO + s * sub:HALO + (s + 1) * sub, :] = hb
            hl = (h - hb.astype(F32)).astype(BF16)
            dt = (lax.dot_general(wdth_ref[...], hb, nt_dims, preferred_element_type=F32)
                  + lax.dot_general(wdth_ref[...], hl, nt_dims, preferred_element_type=F32)
                  + lax.dot_general(wdtl_ref[...], hb, nt_dims, preferred_element_type=F32))
            for c in range(sub // SSD_CHUNK):
                dt_ref[s * (sub // SSD_CHUNK) + c] = dt[:, c * SSD_CHUNK:(c + 1) * SSD_CHUNK]

    def plain(act):
        for s in range(n_sub):
            rows = h_scr[HALO + s * sub:HALO + (s + 1) * sub, :]
            acc = jnp.dot(rows, w_ref[...], preferred_element_type=F32)
            out_ref[s * sub:(s + 1) * sub, :] = act(acc).astype(BF16)

    is_silu = (j == BLK_Z) | (j == BLK_Z + 1) | (j == BLK_GATE) | (j == BLK_XGATE)
    is_conv = (j >= BLK_XS) & (j <= BLK_CM)
    is_sig = j >= BLK_MERGE

    @pl.when(is_silu)
    def _():
        plain(_silu)

    @pl.when(j == BLK_XQ)
    def _():
        plain(lambda a: a)

    @pl.when(j == BLK_U)
    def _():
        plain(_gelu_tanh)

    @pl.when(j == BLK_V)
    def _():
        def gelu_ln(a):
            v = _gelu_tanh(a)
            mu = jnp.mean(v, axis=-1, keepdims=True)
            vc = v - mu
            return vc * lax.rsqrt(jnp.mean(vc * vc, axis=-1, keepdims=True) + EPS) * lng_ref[...] + lnb_ref[...]
        plain(gelu_ln)

    @pl.when(is_sig)
    def _():
        plain(jax.nn.sigmoid)

    @pl.when(is_conv)
    def _():
        for s in range(n_sub):
            rows = h_scr[s * sub:(s + 1) * sub + 2 * HALO, :]
            acc_scr[...] = jnp.dot(rows, w_ref[...], preferred_element_type=F32)
            y = convb_ref[...]
            for k in range(SSD_CONV):
                off = HALO - CONV_PAD + k
                y = y + convw_ref[k:k + 1, :] * acc_scr[off:off + sub, :]
            out_ref[s * sub:(s + 1) * sub, :] = _silu(y).astype(BF16)


def _inproj(x2, gpre, w_main, wdt_hi, wdt_lo, convw, convb, lng, lnb, *, seq_len, tm=2048, sub=512):
    t = x2.shape[0]
    assert t % tm == 0 and seq_len % tm == 0 and tm % sub == 0 and sub % SSD_CHUNK == 0
    n_tiles = t // tm
    halo_blocks = t // HALO
    per_tile = tm // HALO
    kernel = functools.partial(_inproj_kernel, tm=tm, sub=sub, tiles_per_seq=seq_len // tm)
    conv_idx = lambda i, j: (0, jnp.clip(j - BLK_XS, 0, BLK_CM - BLK_XS))
    return pl.pallas_call(
        kernel,
        grid=(n_tiles, N_COL_BLOCKS),
        in_specs=[
            pl.BlockSpec((HALO, D_MODEL), lambda i, j: (jnp.maximum(i * per_tile - 1, 0), 0)),
            pl.BlockSpec((tm, D_MODEL), lambda i, j: (i, 0)),
            pl.BlockSpec((HALO, D_MODEL), lambda i, j: (jnp.minimum((i + 1) * per_tile, halo_blocks - 1), 0)),
            pl.BlockSpec((1, D_MODEL), lambda i, j: (0, 0)),
            pl.BlockSpec((D_MODEL, COL_BLOCK), lambda i, j: (0, j)),
            pl.BlockSpec((2 * SSD_HEADS, D_MODEL), lambda i, j: (0, 0)),
            pl.BlockSpec((2 * SSD_HEADS, D_MODEL), lambda i, j: (0, 0)),
            pl.BlockSpec((SSD_CONV, COL_BLOCK), conv_idx),
            pl.BlockSpec((1, COL_BLOCK), conv_idx),
            pl.BlockSpec((1, COL_BLOCK), lambda i, j: (0, 0)),
            pl.BlockSpec((1, COL_BLOCK), lambda i, j: (0, 0)),
        ],
        out_specs=[
            pl.BlockSpec((tm, COL_BLOCK), lambda i, j: (i, j)),
            pl.BlockSpec((tm // SSD_CHUNK, 2 * SSD_HEADS, SSD_CHUNK), lambda i, j: (i, 0, 0)),
        ],
        out_shape=[
            jax.ShapeDtypeStruct((t, PROJ_COLS), BF16),
            jax.ShapeDtypeStruct((t // SSD_CHUNK, 2 * SSD_HEADS, SSD_CHUNK), F32),
        ],
        scratch_shapes=[
            pltpu.VMEM((tm + 2 * HALO, D_MODEL), BF16),
            pltpu.VMEM((sub + 2 * HALO, COL_BLOCK), F32),
        ],
        compiler_params=pltpu.CompilerParams(
            dimension_semantics=("parallel", "arbitrary"),
            vmem_limit_bytes=48 * 1024 * 1024),
        name="inproj",
    )(x2, x2, x2, gpre, w_main, wdt_hi, wdt_lo, convw, convb, lng, lnb)


TRI_BLOCKS = 5
EXPAND_BLOCKS = 4
COL_CS, COL_ECS, COL_W = 0, DT_ROWS, 2 * DT_ROWS


def _softplus(x):
    return jnp.maximum(x, 0.0) + jnp.log1p(jnp.exp(-jnp.abs(x)))


def _dt_rows(dt_raw, dt_bias, a, tri):
    rows, q = dt_raw.shape
    dt = _softplus(dt_raw + dt_bias)
    da = dt * a
    hi = da.astype(BF16).astype(F32)
    r1 = da - hi
    mid = r1.astype(BF16).astype(F32)
    lo = r1 - mid
    parts = [hi, mid, lo]
    if (3 * rows) % BF16_ROWS:
        parts.append(jnp.zeros((BF16_ROWS - (3 * rows) % BF16_ROWS, q), F32))
    sums = jnp.dot(jnp.concatenate(parts, axis=0).astype(BF16), tri, preferred_element_type=F32)
    s = sums[0:rows] + sums[rows:2 * rows] + sums[2 * rows:3 * rows]
    fwd = lax.broadcasted_iota(jnp.int32, (rows, q), 0) % DT_ROWS < HEADS_PER_GROUP
    cs = jnp.where(fwd, s[:, 0:q], s[:, q:2 * q])
    rem = jnp.where(fwd, s[:, 2 * q:3 * q], s[:, 3 * q:4 * q])
    total = s[:, 4 * q:5 * q]
    return dt, cs, jnp.exp(cs), dt * jnp.exp(rem), jnp.exp(total)


def _token_columns(cs, ecs, w):
    pad = jnp.zeros((SSD_CHUNK - 3 * DT_ROWS, SSD_CHUNK), F32)
    return jnp.transpose(jnp.concatenate([cs, ecs, w, pad], axis=0))


def _head_row(v, d):
    lane = lax.broadcasted_iota(jnp.int32, (1, LANES), 1)
    r = d * HEADS_PER_GROUP
    lo = jnp.where(lane < SSD_HEAD_DIM, v[r:r + 1], v[r + 1:r + 2])
    hi = jnp.where(lane < SSD_HEAD_DIM, v[r + 2:r + 3], v[r + 3:r + 4])
    return jnp.concatenate([lo, hi], axis=1)


def _ssd_kernel(xs0_ref, bm0_ref, dt0_ref, dtb0_ref, alog0_ref,
                xs_ref, bm_ref, cm_ref, z_ref, dt_ref, dtb_ref, alog_ref, dskip_ref, ng_ref,
                tri_ref, exp_ref, y_ref, hb_all, hf_scr, hb_scr, *, cpb):
    k = pl.program_id(0)
    i = pl.program_id(1)
    nb = pl.num_programs(1)
    q = SSD_CHUNK
    gw = GROUP_WIDTH
    nr = cpb * DT_ROWS
    chunks = range(cpb)
    rows_of = lambda c: slice(c * q, (c + 1) * q)
    heads_of = lambda v, c: v[c * DT_ROWS:(c + 1) * DT_ROWS]
    wslot = k % 2
    rslot = 1 - wslot

    @pl.when((k == 0) & (i == 0))
    def _():
        hb_all[1] = jnp.zeros(hb_all.shape[1:], hb_all.dtype)

    @pl.when(i == 0)
    def _():
        hb_scr[...] = jnp.zeros_like(hb_scr)
        hf_scr[...] = jnp.zeros_like(hf_scr)

    nt_dims = (((1,), (1,)), ((), ()))
    scores = [lax.dot_general(cm_ref[rows_of(c), :], bm_ref[rows_of(c), :], nt_dims,
                              preferred_element_type=F32) for c in chunks]
    tile = lambda v: jnp.concatenate([v] * cpb, axis=0)
    dt_raw = jnp.concatenate([dt0_ref[...].reshape(nr, q), dt_ref[...].reshape(nr, q)], axis=0)
    bias = jnp.concatenate([tile(dtb0_ref[0]), tile(dtb_ref[0])], axis=0)
    a = jnp.concatenate([tile(-jnp.exp(alog0_ref[0])), tile(-jnp.exp(alog_ref[0]))], axis=0)
    dt, cs, ecs, w, cd = _dt_rows(dt_raw, bias, a, tri_ref[...])
    cols = [_token_columns(heads_of(cs, c), heads_of(ecs, c), heads_of(w, c)) for c in range(2 * cpb)]
    cols0, cols = cols[:cpb], cols[cpb:]
    cd0 = cd[:nr]
    dt, cs, cd = (v[nr:] for v in (dt, cs, cd))

    wexp0 = jnp.dot(jnp.concatenate(cols0, axis=0).astype(BF16), exp_ref[:, 3 * gw:4 * gw],
                    preferred_element_type=F32)
    expd = jnp.dot(jnp.concatenate(cols, axis=0).astype(BF16), exp_ref[...],
                   preferred_element_type=F32)

    def local_state(bm, xs, wexp):
        bmt = jnp.transpose(bm.astype(F32)).astype(BF16)
        return jnp.dot(bmt, (xs.astype(F32) * wexp).astype(BF16), preferred_element_type=F32)

    st0 = [local_state(bm0_ref[rows_of(c), :], xs0_ref[rows_of(c), :], wexp0[rows_of(c)]) for c in chunks]
    h = hb_scr[...]
    for c in reversed(chunks):
        hb_all[wslot, (nb - 1 - i) * cpb + c] = h.astype(BF16)
        h = h * _head_row(heads_of(cd0, c), 1) + st0[c]
    hb_scr[...] = h

    t_idx = lax.broadcasted_iota(jnp.int32, (q, q), 0)
    s_idx = lax.broadcasted_iota(jnp.int32, (q, q), 1)
    lower = s_idx < t_idx
    diag = s_idx == t_idx
    lane_head = lax.broadcasted_iota(jnp.int32, (q, gw), 1) // SSD_HEAD_DIM

    def intra_chunk(c, sc):
        dt_c, cs_c = heads_of(dt, c), heads_of(cs, c)
        xs_b = xs_ref[rows_of(c), :]
        cols_c = cols[c]
        y = jnp.zeros((q, gw), F32)
        for r in range(HEADS_PER_GROUP):
            rb = HEADS_PER_GROUP + r
            colf = jnp.broadcast_to(cols_c[:, COL_CS + r:COL_CS + r + 1], (q, q))
            colb = jnp.broadcast_to(cols_c[:, COL_CS + rb:COL_CS + rb + 1], (q, q))
            arg = jnp.where(lower, colf - cs_c[r:r + 1], colb - cs_c[rb:rb + 1])
            dsel = jnp.where(lower, dt_c[r:r + 1],
                             jnp.where(diag, dt_c[r:r + 1] + dt_c[rb:rb + 1], dt_c[rb:rb + 1]))
            m = (sc * jnp.exp(arg) * dsel).astype(BF16)
            x_r = jnp.where(lane_head == r, xs_b, jnp.zeros_like(xs_b))
            y = y + jnp.dot(m, x_r, preferred_element_type=F32)
        return y

    def finish(c, y, yoff):
        ex = expd[rows_of(c)]
        y = y + yoff[:, 0:gw] * ex[:, 0:gw] + yoff[:, gw:2 * gw] * ex[:, gw:2 * gw]
        y = y + dskip_ref[...] * xs_ref[rows_of(c), :].astype(F32)
        y = y * z_ref[rows_of(c), :].astype(F32)
        y = y * lax.rsqrt(jnp.mean(y * y, axis=-1, keepdims=True) + EPS) * ng_ref[...]
        y_ref[rows_of(c), :] = y.astype(BF16)

    st = [local_state(bm_ref[rows_of(c), :], xs_ref[rows_of(c), :], expd[rows_of(c), 2 * gw:3 * gw])
          for c in chunks]
    h = hf_scr[...]
    yoff = []
    for c in chunks:
        hcat = jnp.concatenate([h.astype(BF16), hb_all[rslot, i * cpb + c]], axis=1)
        yoff.append(jnp.dot(cm_ref[rows_of(c), :], hcat, preferred_element_type=F32))
        h = h * _head_row(heads_of(cd, c), 0) + st[c]
    hf_scr[...] = h
    for c in chunks:
        finish(c, intra_chunk(c, scores[c]), yoff[c])


def _ssd_constants():
    q = SSD_CHUNK
    u = jnp.arange(q)[:, None]
    s = jnp.arange(q)[None, :]
    tri = jnp.concatenate([u <= s, u >= s, u > s, u < s, jnp.ones((q, q), bool)], axis=1).astype(BF16)
    row = jnp.arange(q)[:, None]
    col = jnp.arange(EXPAND_BLOCKS * GROUP_WIDTH)[None, :]
    src = jnp.array([COL_ECS, COL_ECS + HEADS_PER_GROUP, COL_W, COL_W + HEADS_PER_GROUP])[col // GROUP_WIDTH]
    expand = (row == src + (col % GROUP_WIDTH) // SSD_HEAD_DIM).astype(BF16)
    return tri, expand


def _ssd(proj, dt4, dtb, alog, dskip, ng, *, batch, seq_len, tq=1024):
    t = proj.shape[0]
    assert seq_len % tq == 0 and tq % SSD_CHUNK == 0
    nb = seq_len // tq
    cpb = tq // SSD_CHUNK
    tri, expand = _ssd_constants()
    gw = GROUP_WIDTH
    xs0, bm0, cm0 = BLK_XS * COL_BLOCK // gw, BLK_BM * COL_BLOCK // SSD_STATE, BLK_CM * COL_BLOCK // SSD_STATE

    n_pairs = batch * SSD_GROUPS

    def back(k):
        p = jnp.minimum(k, n_pairs - 1)
        return p // SSD_GROUPS, p % SSD_GROUPS

    def out(k):
        p = jnp.maximum(k - 1, 0)
        return p // SSD_GROUPS, p % SSD_GROUPS

    def back_rows(k, i):
        return back(k)[0] * nb + nb - 1 - i

    def out_rows(k, i):
        return out(k)[0] * nb + i

    def dt_specs(pair):
        rows = back_rows if pair is back else out_rows
        grp = (1, DT_ROWS, LANES)
        return [
            pl.BlockSpec((cpb, DT_ROWS, SSD_CHUNK), lambda k, i: (rows(k, i), pair(k)[1], 0)),
            pl.BlockSpec(grp, lambda k, i: (pair(k)[1], 0, 0)),
            pl.BlockSpec(grp, lambda k, i: (pair(k)[1], 0, 0)),
        ]

    const2 = lambda k, i: (0, 0)
    return pl.pallas_call(
        functools.partial(_ssd_kernel, cpb=cpb),
        grid=(n_pairs + 1, nb),
        in_specs=[
            pl.BlockSpec((tq, gw), lambda k, i: (back_rows(k, i), xs0 + back(k)[1])),
            pl.BlockSpec((tq, SSD_STATE), lambda k, i: (back_rows(k, i), bm0 + back(k)[1])),
            *dt_specs(back),
            pl.BlockSpec((tq, gw), lambda k, i: (out_rows(k, i), xs0 + out(k)[1])),
            pl.BlockSpec((tq, SSD_STATE), lambda k, i: (out_rows(k, i), bm0 + out(k)[1])),
            pl.BlockSpec((tq, SSD_STATE), lambda k, i: (out_rows(k, i), cm0 + out(k)[1])),
            pl.BlockSpec((tq, gw), lambda k, i: (out_rows(k, i), out(k)[1])),
            *dt_specs(out),
            pl.BlockSpec((1, gw), lambda k, i: (0, out(k)[1])),
            pl.BlockSpec((1, gw), lambda k, i: (0, out(k)[1])),
            pl.BlockSpec((SSD_CHUNK, TRI_BLOCKS * SSD_CHUNK), const2),
            pl.BlockSpec((SSD_CHUNK, EXPAND_BLOCKS * gw), const2),
        ],
        out_specs=pl.BlockSpec((tq, gw), lambda k, i: (jnp.where(k == 0, t // tq, out_rows(k, i)), out(k)[1])),
        out_shape=jax.ShapeDtypeStruct((t + tq, SSD_WIDTH), BF16),
        scratch_shapes=[
            pltpu.VMEM((2, seq_len // SSD_CHUNK, SSD_STATE, gw), BF16),
            pltpu.VMEM((SSD_STATE, gw), F32),
            pltpu.VMEM((SSD_STATE, gw), F32),
        ],
        compiler_params=pltpu.CompilerParams(
            dimension_semantics=("arbitrary", "arbitrary"),
            vmem_limit_bytes=48 * 1024 * 1024),
        name="ssd",
    )(proj, proj, dt4, dtb, alog, proj, proj, proj, proj, dt4, dtb, alog, dskip, ng, tri, expand)


def _memkv_kernel(mem_ref, g_ref, w_ref, kv_ref):
    m = _rms(mem_ref[0], g_ref[...]).astype(BF16)
    kv_ref[0] = jnp.dot(m, w_ref[...], preferred_element_type=F32).astype(BF16)


def _memkv(mem, g, w_kv):
    b = mem.shape[0]
    return pl.pallas_call(
        _memkv_kernel,
        grid=(b,),
        in_specs=[
            pl.BlockSpec((1, MEM_LEN, D_MODEL), lambda i: (i, 0, 0)),
            pl.BlockSpec((1, D_MODEL), lambda i: (0, 0)),
            pl.BlockSpec((D_MODEL, 2 * D_MODEL), lambda i: (0, 0)),
        ],
        out_specs=pl.BlockSpec((1, MEM_LEN, 2 * D_MODEL), lambda i: (i, 0, 0)),
        out_shape=jax.ShapeDtypeStruct((b, MEM_LEN, 2 * D_MODEL), BF16),
        compiler_params=pltpu.CompilerParams(dimension_semantics=("parallel",)),
        name="memkv",
    )(mem, g, w_kv)


def _tail_kernel(yssd_ref, gate_ref, u_ref, v_ref, xq_ref, xgate_ref, m0_ref, m1_ref, m2_ref, x_ref,
                 k_ref, vmem_ref, ws_ref, bexp_ref, wbs_ref, wbg_ref, wbx_ref, wout_ref, gpost_ref,
                 out_ref, sv_scr, o_scr, *, tq):
    n_chunks = tq // GMLP_CHUNK
    gc = GMLP_CHUNK
    gwid = GMLP_GROUP_WIDTH

    for g in range(GMLP_GROUPS):
        cols = slice(g * gwid, (g + 1) * gwid)
        vcat = jnp.concatenate([v_ref[c * gc:(c + 1) * gc, cols] for c in range(n_chunks)], axis=1)
        sv = jnp.dot(ws_ref[g], vcat, preferred_element_type=F32)
        for c in range(n_chunks):
            sv_scr[c * gc:(c + 1) * gc, cols] = sv[:, c * gc:(c + 1) * gc]
    for c in range(n_chunks):
        rows = slice(c * gc, (c + 1) * gc)
        sv_scr[rows, :] = (u_ref[rows, :].astype(F32) * (sv_scr[rows, :] + bexp_ref[...])
                           * gate_ref[rows, :].astype(F32))

    nt_dims = (((1,), (1,)), ((), ()))
    scale = XATTN_HEAD_DIM ** -0.5
    for h in range(XATTN_HEADS):
        hc = slice(h * XATTN_HEAD_DIM, (h + 1) * XATTN_HEAD_DIM)
        s = lax.dot_general(xq_ref[:, hc], k_ref[0, :, hc], nt_dims, preferred_element_type=F32) * scale
        e = jnp.exp(s - jnp.max(s, axis=-1, keepdims=True))
        p = e / jnp.sum(e, axis=-1, keepdims=True)
        o = jnp.dot(p.astype(BF16), vmem_ref[0, :, hc], preferred_element_type=F32)
        o_scr[:, hc] = o * xgate_ref[:, hc].astype(F32)

    merged = m0_ref[...].astype(F32) * jnp.dot(yssd_ref[...], wbs_ref[...], preferred_element_type=F32)
    merged = merged + m1_ref[...].astype(F32) * jnp.dot(sv_scr[...].astype(BF16), wbg_ref[...],
                                                        preferred_element_type=F32)
    merged = merged + m2_ref[...].astype(F32) * jnp.dot(o_scr[...].astype(BF16), wbx_ref[...],
                                                        preferred_element_type=F32)
    out = jnp.dot(merged.astype(BF16), wout_ref[...], preferred_element_type=F32)
    out_ref[...] = x_ref[...] + _rms(out, gpost_ref[...])


def _tail(y_ssd, proj, x2, kv, ws, bexp, wbs, wbg, wbx, wout, gpost, *, seq_len, tq=512):
    t = x2.shape[0]
    assert seq_len % tq == 0 and tq % GMLP_CHUNK == 0
    per_seq = seq_len // tq
    resident = functools.partial(pl.BlockSpec, pipeline_mode=pl.Buffered(1))

    def proj_block(blk):
        return pl.BlockSpec((tq, COL_BLOCK), lambda i: (i, blk))

    return pl.pallas_call(
        functools.partial(_tail_kernel, tq=tq),
        grid=(t // tq,),
        in_specs=[
            pl.BlockSpec((tq, SSD_WIDTH), lambda i: (i, 0)),
            proj_block(BLK_GATE), proj_block(BLK_U), proj_block(BLK_V), proj_block(BLK_XQ), proj_block(BLK_XGATE),
            proj_block(BLK_MERGE), proj_block(BLK_MERGE + 1), proj_block(BLK_MERGE + 2),
            pl.BlockSpec((tq, D_MODEL), lambda i: (i, 0)),
            pl.BlockSpec((1, MEM_LEN, D_MODEL), lambda i: (i // per_seq, 0, 0)),
            pl.BlockSpec((1, MEM_LEN, D_MODEL), lambda i: (i // per_seq, 0, 1)),
            resident((GMLP_GROUPS, GMLP_CHUNK, GMLP_CHUNK), lambda i: (0, 0, 0)),
            resident((GMLP_CHUNK, GMLP_WIDTH), lambda i: (0, 0)),
            resident((SSD_WIDTH, D_MODEL), lambda i: (0, 0)),
            resident((GMLP_WIDTH, D_MODEL), lambda i: (0, 0)),
            resident((D_MODEL, D_MODEL), lambda i: (0, 0)),
            resident((D_MODEL, D_MODEL), lambda i: (0, 0)),
            resident((1, D_MODEL), lambda i: (0, 0)),
        ],
        out_specs=pl.BlockSpec((tq, D_MODEL), lambda i: (i, 0)),
        out_shape=jax.ShapeDtypeStruct((t, D_MODEL), F32),
        scratch_shapes=[
            pltpu.VMEM((tq, GMLP_WIDTH), F32),
            pltpu.VMEM((tq, D_MODEL), F32),
        ],
        compiler_params=pltpu.CompilerParams(
            dimension_semantics=("parallel",),
            vmem_limit_bytes=56 * 1024 * 1024),
        name="tail",
    )(y_ssd, proj, proj, proj, proj, proj, proj, proj, proj, x2, kv, kv,
      ws, bexp, wbs, wbg, wbx, wout, gpost)


def _split_bf16(w):
    hi = w.astype(BF16)
    return hi, (w - hi.astype(F32)).astype(BF16)


def _prep_layer(p, l):
    w_in = p['w_in'][l]
    dt0 = SSD_WIDTH + SSD_WIDTH + 2 * SSD_GROUPS * SSD_STATE
    dt1 = dt0 + 2 * SSD_HEADS
    w_main = jnp.concatenate([w_in[:, :dt0], w_in[:, dt1:]], axis=1).astype(BF16)
    w_dt = w_in[:, dt0:dt1].reshape(D_MODEL, 2, SSD_GROUPS, HEADS_PER_GROUP)
    w_dt = w_dt.transpose(2, 1, 3, 0).reshape(2 * SSD_HEADS, D_MODEL)
    wdt_hi, wdt_lo = _split_bf16(w_dt)

    def per_group_rows(v):
        v = v.reshape(2, SSD_GROUPS, HEADS_PER_GROUP).transpose(1, 0, 2).reshape(SSD_GROUPS, DT_ROWS, 1)
        return jnp.broadcast_to(v, (SSD_GROUPS, DT_ROWS, LANES))

    return dict(
        dtb=per_group_rows(p['dt_bias'][l]), alog=per_group_rows(p['a_log'][l]),
        dskip=jnp.repeat(p['d_skip'][l], SSD_HEAD_DIM).reshape(1, SSD_WIDTH),
        ssd_ng=p['ssd_norm_g'][l].reshape(1, SSD_WIDTH),
        gpre=p['norm_pre_g'][l].reshape(1, D_MODEL),
        w_main=w_main, wdt_hi=wdt_hi, wdt_lo=wdt_lo,
        convw=p['conv_w'][l], convb=p['conv_b'][l].reshape(1, -1),
        lng=p['gmlp_ln_g'][l].reshape(1, -1), lnb=p['gmlp_ln_b'][l].reshape(1, -1),
        ws=p['w_spatial'][l].astype(BF16),
        bexp=jnp.repeat(p['b_spatial'][l].T, GMLP_GROUP_WIDTH, axis=1),
        mem_g=p['mem_norm_g'][l].reshape(1, D_MODEL), w_kv=p['w_kv'][l].astype(BF16),
        wbs=p['w_br_ssd'][l].astype(BF16), wbg=p['w_br_gmlp'][l].astype(BF16),
        wbx=p['w_br_xattn'][l].astype(BF16), wout=p['w_out'][l].astype(BF16),
        gpost=p['norm_post_g'][l].reshape(1, D_MODEL),
    )


def _layer(x2, mem, prm, *, batch, seq_len):
    proj, dt4 = _inproj(x2, prm['gpre'], prm['w_main'], prm['wdt_hi'], prm['wdt_lo'],
                        prm['convw'], prm['convb'], prm['lng'], prm['lnb'], seq_len=seq_len)
    y_ssd = _ssd(proj, dt4, prm['dtb'], prm['alog'], prm['dskip'], prm['ssd_ng'], batch=batch, seq_len=seq_len)
    kv = _memkv(mem, prm['mem_g'], prm['w_kv'])
    return _tail(y_ssd, proj, x2, kv, prm['ws'], prm['bexp'], prm['wbs'], prm['wbg'], prm['wbx'], prm['wout'],
                 prm['gpost'], seq_len=seq_len)


def kernel(x, mem, norm_pre_g, w_in, conv_w, conv_b, dt_bias, a_log, d_skip, ssd_norm_g, gmlp_ln_g, gmlp_ln_b, w_spatial, b_spatial, mem_norm_g, w_kv, w_br_ssd, w_br_gmlp, w_br_xattn, w_out, norm_post_g):
    p = dict(norm_pre_g=norm_pre_g, w_in=w_in, conv_w=conv_w, conv_b=conv_b, dt_bias=dt_bias, a_log=a_log,
             d_skip=d_skip, ssd_norm_g=ssd_norm_g, gmlp_ln_g=gmlp_ln_g, gmlp_ln_b=gmlp_ln_b, w_spatial=w_spatial,
             b_spatial=b_spatial, mem_norm_g=mem_norm_g, w_kv=w_kv, w_br_ssd=w_br_ssd, w_br_gmlp=w_br_gmlp,
             w_br_xattn=w_br_xattn, w_out=w_out, norm_post_g=norm_post_g)
    batch, seq_len, _ = x.shape
    x2 = x.reshape(batch * seq_len, D_MODEL)
    for l in range(w_in.shape[0]):
        x2 = _layer(x2, mem, _prep_layer(p, l), batch=batch, seq_len=seq_len)
    return x2.reshape(batch, seq_len, D_MODEL)
```

```python
import functools

import jax
import jax.numpy as jnp
from jax import lax
from jax.experimental import pallas as pl
from jax.experimental.pallas import tpu as pltpu

F32 = jnp.float32
BF16 = jnp.bfloat16

EPS = 1e-6
D_MODEL = 1024
N_BRANCH = 3

SSD_WIDTH = 2 * D_MODEL
SSD_HEAD_DIM = 64
SSD_HEADS = SSD_WIDTH // SSD_HEAD_DIM
SSD_GROUPS = 8
SSD_STATE = 128
SSD_CONV = 5
SSD_CHUNK = 128
HEADS_PER_GROUP = SSD_HEADS // SSD_GROUPS
GROUP_WIDTH = HEADS_PER_GROUP * SSD_HEAD_DIM
DT_ROWS = 2 * HEADS_PER_GROUP

GMLP_WIDTH = D_MODEL
GMLP_GROUPS = 8
GMLP_CHUNK = 128
GMLP_GROUP_WIDTH = GMLP_WIDTH // GMLP_GROUPS

XATTN_HEADS = 4
XATTN_HEAD_DIM = D_MODEL // XATTN_HEADS
MEM_LEN = 256

LANES = 128
BF16_ROWS = 16

COL_BLOCK = 1024
BLK_Z, BLK_XS, BLK_BM, BLK_CM, BLK_GATE, BLK_U, BLK_V, BLK_XQ, BLK_XGATE, BLK_MERGE = 0, 2, 4, 5, 6, 7, 8, 9, 10, 11
N_COL_BLOCKS = 14
PROJ_COLS = N_COL_BLOCKS * COL_BLOCK
HALO = BF16_ROWS
CONV_PAD = SSD_CONV // 2


def _silu(x):
    return x * jax.nn.sigmoid(x)


def _gelu_tanh(x):
    c = 0.7978845608028654
    return x * (0.5 * (1.0 + jnp.tanh(c * (x + 0.044715 * (x * x * x)))))


def _rms(x, g):
    return x * lax.rsqrt(jnp.mean(x * x, axis=-1, keepdims=True) + EPS) * g


def _inproj_kernel(xprev_ref, x_ref, xnext_ref, gpre_ref, w_ref, wdth_ref, wdtl_ref,
                   convw_ref, convb_ref, lng_ref, lnb_ref,
                   out_ref, dt_ref, h_scr, h32_scr, hperm_scr, ynat_scr, *, tm, sub, tiles_per_seq):
    i = pl.program_id(0)
    j = pl.program_id(1)
    n_sub = tm // sub
    n_lb = D_MODEL // LANES
    ns = (sub + 2 * HALO) // 8
    nt_dims = (((1,), (1,)), ((), ()))

    def put_h32(row0, h):
        for lb in range(n_lb):
            h32_scr[lb, row0:row0 + h.shape[0], :] = h[:, lb * LANES:(lb + 1) * LANES]

    @pl.when(j == 0)
    def _():
        g = gpre_ref[...]
        first = (i % tiles_per_seq) == 0
        last = (i % tiles_per_seq) == tiles_per_seq - 1
        put_h32(0, _rms(xprev_ref[...], g) * jnp.where(first, 0.0, 1.0))
        put_h32(HALO + tm, _rms(xnext_ref[...], g) * jnp.where(last, 0.0, 1.0))
        for s in range(n_sub):
            h = _rms(x_ref[s * sub:(s + 1) * sub, :], g)
            hb = h.astype(BF16)
            h_scr[s * sub:(s + 1) * sub, :] = hb
            put_h32(HALO + s * sub, h)
            hl = (h - hb.astype(F32)).astype(BF16)
            dt = (lax.dot_general(wdth_ref[...], hb, nt_dims, preferred_element_type=F32)
                  + lax.dot_general(wdth_ref[...], hl, nt_dims, preferred_element_type=F32)
                  + lax.dot_general(wdtl_ref[...], hb, nt_dims, preferred_element_type=F32))
            for c in range(sub // SSD_CHUNK):
                dt_ref[s * (sub // SSD_CHUNK) + c] = dt[:, c * SSD_CHUNK:(c + 1) * SSD_CHUNK]
        for s in range(n_sub):
            for b in range(0, ns, 2):
                slabs = [jnp.concatenate([h32_scr[lb, pl.ds(s * sub + b + d, 8, stride=ns), :]
                                          for lb in range(n_lb)], axis=1) for d in range(2)]
                hperm_scr[s, 8 * b:8 * b + BF16_ROWS, :] = jnp.concatenate(slabs, axis=0).astype(BF16)

    def plain(act):
        for s in range(n_sub):
            rows = h_scr[s * sub:(s + 1) * sub, :]
            acc = jnp.dot(rows, w_ref[...], preferred_element_type=F32)
            out_ref[s * sub:(s + 1) * sub, :] = act(acc).astype(BF16)

    is_silu = (j == BLK_Z) | (j == BLK_Z + 1) | (j == BLK_GATE) | (j == BLK_XGATE)
    is_conv = (j >= BLK_XS) & (j <= BLK_CM)
    is_sig = j >= BLK_MERGE

    @pl.when(is_silu)
    def _():
        plain(_silu)

    @pl.when(j == BLK_XQ)
    def _():
        plain(lambda a: a)

    @pl.when(j == BLK_U)
    def _():
        plain(_gelu_tanh)

    @pl.when(j == BLK_V)
    def _():
        def gelu_ln(a):
            v = _gelu_tanh(a)
            mu = jnp.mean(v, axis=-1, keepdims=True)
            vc = v - mu
            return vc * lax.rsqrt(jnp.mean(vc * vc, axis=-1, keepdims=True) + EPS) * lng_ref[...] + lnb_ref[...]
        plain(gelu_ln)

    @pl.when(is_sig)
    def _():
        plain(jax.nn.sigmoid)

    @pl.when(is_conv)
    def _():
        taps = [convw_ref[k:k + 1, :].reshape(1, 1, COL_BLOCK) for k in range(SSD_CONV)]
        project = lambda s: jnp.dot(hperm_scr[s], w_ref[...], preferred_element_type=F32).reshape(ns, 8, COL_BLOCK)
        nxt = project(0)
        for s in range(n_sub):
            acc = nxt
            if s + 1 < n_sub:
                nxt = project(s + 1)
            ext = jnp.concatenate([pltpu.roll(acc[ns - CONV_PAD:ns], 1, 1), acc,
                                   pltpu.roll(acc[0:CONV_PAD], 8 - 1, 1)], axis=0)
            y = convb_ref[...].reshape(1, 1, COL_BLOCK)
            for k in range(SSD_CONV):
                y = y + taps[k] * ext[k:k + ns]
            for b in range(ns):
                for lb in range(n_lb):
                    ynat_scr[s % 2, lb, pl.ds(b, 8, stride=ns), :] = y[b][:, lb * LANES:(lb + 1) * LANES]
            y = jnp.concatenate([ynat_scr[s % 2, lb, HALO:HALO + sub, :] for lb in range(n_lb)], axis=1)
            out_ref[s * sub:(s + 1) * sub, :] = _silu(y).astype(BF16)


def _inproj(x2, gpre, w_main, wdt_hi, wdt_lo, convw, convb, lng, lnb, *, seq_len, tm=2048, sub=512):
    t = x2.shape[0]
    assert t % tm == 0 and seq_len % tm == 0 and tm % sub == 0 and sub % SSD_CHUNK == 0
    win = sub + 2 * HALO
    assert win % 8 == 0 and (win // 8) % 8 != 0
    n_tiles = t // tm
    halo_blocks = t // HALO
    per_tile = tm // HALO
    kernel = functools.partial(_inproj_kernel, tm=tm, sub=sub, tiles_per_seq=seq_len // tm)
    conv_idx = lambda i, j: (0, jnp.clip(j - BLK_XS, 0, BLK_CM - BLK_XS))
    return pl.pallas_call(
        kernel,
        grid=(n_tiles, N_COL_BLOCKS),
        in_specs=[
            pl.BlockSpec((HALO, D_MODEL), lambda i, j: (jnp.maximum(i * per_tile - 1, 0), 0)),
            pl.BlockSpec((tm, D_MODEL), lambda i, j: (i, 0)),
            pl.BlockSpec((HALO, D_MODEL), lambda i, j: (jnp.minimum((i + 1) * per_tile, halo_blocks - 1), 0)),
            pl.BlockSpec((1, D_MODEL), lambda i, j: (0, 0)),
            pl.BlockSpec((D_MODEL, COL_BLOCK), lambda i, j: (0, j)),
            pl.BlockSpec((2 * SSD_HEADS, D_MODEL), lambda i, j: (0, 0)),
            pl.BlockSpec((2 * SSD_HEADS, D_MODEL), lambda i, j: (0, 0)),
            pl.BlockSpec((SSD_CONV, COL_BLOCK), conv_idx),
            pl.BlockSpec((1, COL_BLOCK), conv_idx),
            pl.BlockSpec((1, COL_BLOCK), lambda i, j: (0, 0)),
            pl.BlockSpec((1, COL_BLOCK), lambda i, j: (0, 0)),
        ],
        out_specs=[
            pl.BlockSpec((tm, COL_BLOCK), lambda i, j: (i, j)),
            pl.BlockSpec((tm // SSD_CHUNK, 2 * SSD_HEADS, SSD_CHUNK), lambda i, j: (i, 0, 0)),
        ],
        out_shape=[
            jax.ShapeDtypeStruct((t, PROJ_COLS), BF16),
            jax.ShapeDtypeStruct((t // SSD_CHUNK, 2 * SSD_HEADS, SSD_CHUNK), F32),
        ],
        scratch_shapes=[
            pltpu.VMEM((tm, D_MODEL), BF16),
            pltpu.VMEM((D_MODEL // LANES, tm + 2 * HALO, LANES), F32),
            pltpu.VMEM((tm // sub, win, D_MODEL), BF16),
            pltpu.VMEM((2, COL_BLOCK // LANES, win, LANES), F32),
        ],
        compiler_params=pltpu.CompilerParams(
            dimension_semantics=("parallel", "arbitrary"),
            vmem_limit_bytes=58 * 1024 * 1024),
        name="inproj",
    )(x2, x2, x2, gpre, w_main, wdt_hi, wdt_lo, convw, convb, lng, lnb)


TRI_BLOCKS = 5
EXPAND_BLOCKS = 4
COL_CS, COL_ECS, COL_W = 0, DT_ROWS, 2 * DT_ROWS


def _softplus(x):
    return jnp.maximum(x, 0.0) + jnp.log1p(jnp.exp(-jnp.abs(x)))


def _dt_rows(dt_raw, dt_bias, a, tri):
    rows, q = dt_raw.shape
    dt = _softplus(dt_raw + dt_bias)
    da = dt * a
    hi = da.astype(BF16).astype(F32)
    r1 = da - hi
    mid = r1.astype(BF16).astype(F32)
    lo = r1 - mid
    parts = [hi, mid, lo]
    if (3 * rows) % BF16_ROWS:
        parts.append(jnp.zeros((BF16_ROWS - (3 * rows) % BF16_ROWS, q), F32))
    sums = jnp.dot(jnp.concatenate(parts, axis=0).astype(BF16), tri, preferred_element_type=F32)
    s = sums[0:rows] + sums[rows:2 * rows] + sums[2 * rows:3 * rows]
    fwd = lax.broadcasted_iota(jnp.int32, (rows, q), 0) % DT_ROWS < HEADS_PER_GROUP
    cs = jnp.where(fwd, s[:, 0:q], s[:, q:2 * q])
    rem = jnp.where(fwd, s[:, 2 * q:3 * q], s[:, 3 * q:4 * q])
    total = s[:, 4 * q:5 * q]
    return dt, cs, jnp.exp(cs), dt * jnp.exp(rem), jnp.exp(total)


def _token_columns(cs, ecs, w):
    pad = jnp.zeros((SSD_CHUNK - 3 * DT_ROWS, SSD_CHUNK), F32)
    return jnp.transpose(jnp.concatenate([cs, ecs, w, pad], axis=0))


def _head_row(v, d):
    lane = lax.broadcasted_iota(jnp.int32, (1, LANES), 1)
    r = d * HEADS_PER_GROUP
    lo = jnp.where(lane < SSD_HEAD_DIM, v[r:r + 1], v[r + 1:r + 2])
    hi = jnp.where(lane < SSD_HEAD_DIM, v[r + 2:r + 3], v[r + 3:r + 4])
    return jnp.concatenate([lo, hi], axis=1)


def _ssd_kernel(xs0_ref, bm0_ref, dt0_ref, dtb0_ref, alog0_ref,
                xs_ref, bm_ref, cm_ref, z_ref, dt_ref, dtb_ref, alog_ref, dskip_ref, ng_ref,
                tri_ref, exp_ref, y_ref, hb_all, hf_scr, hb_scr, *, cpb):
    k = pl.program_id(0)
    i = pl.program_id(1)
    nb = pl.num_programs(1)
    q = SSD_CHUNK
    gw = GROUP_WIDTH
    nr = cpb * DT_ROWS
    chunks = range(cpb)
    rows_of = lambda c: slice(c * q, (c + 1) * q)
    heads_of = lambda v, c: v[c * DT_ROWS:(c + 1) * DT_ROWS]
    wslot = k % 2
    rslot = 1 - wslot

    @pl.when((k == 0) & (i == 0))
    def _():
        hb_all[1] = jnp.zeros(hb_all.shape[1:], hb_all.dtype)

    @pl.when(i == 0)
    def _():
        hb_scr[...] = jnp.zeros_like(hb_scr)
        hf_scr[...] = jnp.zeros_like(hf_scr)

    nt_dims = (((1,), (1,)), ((), ()))
    scores = [lax.dot_general(cm_ref[rows_of(c), :], bm_ref[rows_of(c), :], nt_dims,
                              preferred_element_type=F32) for c in chunks]
    tile = lambda v: jnp.concatenate([v] * cpb, axis=0)
    dt_raw = jnp.concatenate([dt0_ref[...].reshape(nr, q), dt_ref[...].reshape(nr, q)], axis=0)
    bias = jnp.concatenate([tile(dtb0_ref[0]), tile(dtb_ref[0])], axis=0)
    a = jnp.concatenate([tile(-jnp.exp(alog0_ref[0])), tile(-jnp.exp(alog_ref[0]))], axis=0)
    dt, cs, ecs, w, cd = _dt_rows(dt_raw, bias, a, tri_ref[...])
    cols = [_token_columns(heads_of(cs, c), heads_of(ecs, c), heads_of(w, c)) for c in range(2 * cpb)]
    cols0, cols = cols[:cpb], cols[cpb:]
    cd0 = cd[:nr]
    dt, cs, cd = (v[nr:] for v in (dt, cs, cd))

    wexp0 = jnp.dot(jnp.concatenate(cols0, axis=0).astype(BF16), exp_ref[:, 3 * gw:4 * gw],
                    preferred_element_type=F32)
    expd = jnp.dot(jnp.concatenate(cols, axis=0).astype(BF16), exp_ref[...],
                   preferred_element_type=F32)

    def local_state(bm, xs, wexp):
        bmt = jnp.transpose(bm.astype(F32)).astype(BF16)
        return jnp.dot(bmt, (xs.astype(F32) * wexp).astype(BF16), preferred_element_type=F32)

    st0 = [local_state(bm0_ref[rows_of(c), :], xs0_ref[rows_of(c), :], wexp0[rows_of(c)]) for c in chunks]
    h = hb_scr[...]
    for c in reversed(chunks):
        hb_all[wslot, (nb - 1 - i) * cpb + c] = h.astype(BF16)
        h = h * _head_row(heads_of(cd0, c), 1) + st0[c]
    hb_scr[...] = h

    t_idx = lax.broadcasted_iota(jnp.int32, (q, q), 0)
    s_idx = lax.broadcasted_iota(jnp.int32, (q, q), 1)
    lower = s_idx < t_idx
    diag = s_idx == t_idx
    lane_head = lax.broadcasted_iota(jnp.int32, (q, gw), 1) // SSD_HEAD_DIM

    def intra_chunk(c, sc):
        dt_c, cs_c = heads_of(dt, c), heads_of(cs, c)
        xs_b = xs_ref[rows_of(c), :]
        cols_c = cols[c]
        y = jnp.zeros((q, gw), F32)
        for r in range(HEADS_PER_GROUP):
            rb = HEADS_PER_GROUP + r
            colf = jnp.broadcast_to(cols_c[:, COL_CS + r:COL_CS + r + 1], (q, q))
            colb = jnp.broadcast_to(cols_c[:, COL_CS + rb:COL_CS + rb + 1], (q, q))
            arg = jnp.where(lower, colf - cs_c[r:r + 1], colb - cs_c[rb:rb + 1])
            dsel = jnp.where(lower, dt_c[r:r + 1],
                             jnp.where(diag, dt_c[r:r + 1] + dt_c[rb:rb + 1], dt_c[rb:rb + 1]))
            m = (sc * jnp.exp(arg) * dsel).astype(BF16)
            x_r = jnp.where(lane_head == r, xs_b, jnp.zeros_like(xs_b))
            y = y + jnp.dot(m, x_r, preferred_element_type=F32)
        return y

    def finish(c, y, yoff):
        ex = expd[rows_of(c)]
        y = y + yoff[:, 0:gw] * ex[:, 0:gw] + yoff[:, gw:2 * gw] * ex[:, gw:2 * gw]
        y = y + dskip_ref[...] * xs_ref[rows_of(c), :].astype(F32)
        y = y * z_ref[rows_of(c), :].astype(F32)
        y = y * lax.rsqrt(jnp.mean(y * y, axis=-1, keepdims=True) + EPS) * ng_ref[...]
        y_ref[rows_of(c), :] = y.astype(BF16)

    st = [local_state(bm_ref[rows_of(c), :], xs_ref[rows_of(c), :], expd[rows_of(c), 2 * gw:3 * gw])
          for c in chunks]
    h = hf_scr[...]
    yoff = []
    for c in chunks:
        hcat = jnp.concatenate([h.astype(BF16), hb_all[rslot, i * cpb + c]], axis=1)
        yoff.append(jnp.dot(cm_ref[rows_of(c), :], hcat, preferred_element_type=F32))
        h = h * _head_row(heads_of(cd, c), 0) + st[c]
    hf_scr[...] = h
    for c in chunks:
        finish(c, intra_chunk(c, scores[c]), yoff[c])


def _ssd_constants():
    q = SSD_CHUNK
    u = jnp.arange(q)[:, None]
    s = jnp.arange(q)[None, :]
    tri = jnp.concatenate([u <= s, u >= s, u > s, u < s, jnp.ones((q, q), bool)], axis=1).astype(BF16)
    row = jnp.arange(q)[:, None]
    col = jnp.arange(EXPAND_BLOCKS * GROUP_WIDTH)[None, :]
    src = jnp.array([COL_ECS, COL_ECS + HEADS_PER_GROUP, COL_W, COL_W + HEADS_PER_GROUP])[col // GROUP_WIDTH]
    expand = (row == src + (col % GROUP_WIDTH) // SSD_HEAD_DIM).astype(BF16)
    return tri, expand


def _ssd(proj, dt4, dtb, alog, dskip, ng, *, batch, seq_len, tq=1024):
    t = proj.shape[0]
    assert seq_len % tq == 0 and tq % SSD_CHUNK == 0
    nb = seq_len // tq
    cpb = tq // SSD_CHUNK
    tri, expand = _ssd_constants()
    gw = GROUP_WIDTH
    xs0, bm0, cm0 = BLK_XS * COL_BLOCK // gw, BLK_BM * COL_BLOCK // SSD_STATE, BLK_CM * COL_BLOCK // SSD_STATE

    n_pairs = batch * SSD_GROUPS

    def back(k):
        p = jnp.minimum(k, n_pairs - 1)
        return p // SSD_GROUPS, p % SSD_GROUPS

    def out(k):
        p = jnp.maximum(k - 1, 0)
        return p // SSD_GROUPS, p % SSD_GROUPS

    def back_rows(k, i):
        return back(k)[0] * nb + nb - 1 - i

    def out_rows(k, i):
        return out(k)[0] * nb + i

    def dt_specs(pair):
        rows = back_rows if pair is back else out_rows
        grp = (1, DT_ROWS, LANES)
        return [
            pl.BlockSpec((cpb, DT_ROWS, SSD_CHUNK), lambda k, i: (rows(k, i), pair(k)[1], 0)),
            pl.BlockSpec(grp, lambda k, i: (pair(k)[1], 0, 0)),
            pl.BlockSpec(grp, lambda k, i: (pair(k)[1], 0, 0)),
        ]

    const2 = lambda k, i: (0, 0)
    return pl.pallas_call(
        functools.partial(_ssd_kernel, cpb=cpb),
        grid=(n_pairs + 1, nb),
        in_specs=[
            pl.BlockSpec((tq, gw), lambda k, i: (back_rows(k, i), xs0 + back(k)[1])),
            pl.BlockSpec((tq, SSD_STATE), lambda k, i: (back_rows(k, i), bm0 + back(k)[1])),
            *dt_specs(back),
            pl.BlockSpec((tq, gw), lambda k, i: (out_rows(k, i), xs0 + out(k)[1])),
            pl.BlockSpec((tq, SSD_STATE), lambda k, i: (out_rows(k, i), bm0 + out(k)[1])),
            pl.BlockSpec((tq, SSD_STATE), lambda k, i: (out_rows(k, i), cm0 + out(k)[1])),
            pl.BlockSpec((tq, gw), lambda k, i: (out_rows(k, i), out(k)[1])),
            *dt_specs(out),
            pl.BlockSpec((1, gw), lambda k, i: (0, out(k)[1])),
            pl.BlockSpec((1, gw), lambda k, i: (0, out(k)[1])),
            pl.BlockSpec((SSD_CHUNK, TRI_BLOCKS * SSD_CHUNK), const2),
            pl.BlockSpec((SSD_CHUNK, EXPAND_BLOCKS * gw), const2),
        ],
        out_specs=pl.BlockSpec((tq, gw), lambda k, i: (jnp.where(k == 0, t // tq, out_rows(k, i)), out(k)[1])),
        out_shape=jax.ShapeDtypeStruct((t + tq, SSD_WIDTH), BF16),
        scratch_shapes=[
            pltpu.VMEM((2, seq_len // SSD_CHUNK, SSD_STATE, gw), BF16),
            pltpu.VMEM((SSD_STATE, gw), F32),
            pltpu.VMEM((SSD_STATE, gw), F32),
        ],
        compiler_params=pltpu.CompilerParams(
            dimension_semantics=("arbitrary", "arbitrary"),
            vmem_limit_bytes=48 * 1024 * 1024),
        name="ssd",
    )(proj, proj, dt4, dtb, alog, proj, proj, proj, proj, dt4, dtb, alog, dskip, ng, tri, expand)


def _memkv_kernel(mem_ref, g_ref, w_ref, kv_ref):
    m = _rms(mem_ref[0], g_ref[...]).astype(BF16)
    kv_ref[0] = jnp.dot(m, w_ref[...], preferred_element_type=F32).astype(BF16)


def _memkv(mem, g, w_kv):
    b = mem.shape[0]
    return pl.pallas_call(
        _memkv_kernel,
        grid=(b,),
        in_specs=[
            pl.BlockSpec((1, MEM_LEN, D_MODEL), lambda i: (i, 0, 0)),
            pl.BlockSpec((1, D_MODEL), lambda i: (0, 0)),
            pl.BlockSpec((D_MODEL, 2 * D_MODEL), lambda i: (0, 0)),
        ],
        out_specs=pl.BlockSpec((1, MEM_LEN, 2 * D_MODEL), lambda i: (i, 0, 0)),
        out_shape=jax.ShapeDtypeStruct((b, MEM_LEN, 2 * D_MODEL), BF16),
        compiler_params=pltpu.CompilerParams(dimension_semantics=("parallel",)),
        name="memkv",
    )(mem, g, w_kv)


def _tail_kernel(yssd_ref, gate_ref, u_ref, v_ref, xq_ref, xgate_ref, m0_ref, m1_ref, m2_ref, x_ref,
                 k_ref, vmem_ref, ws_ref, bexp_ref, wbs_ref, wbg_ref, wbx_ref, wout_ref, gpost_ref,
                 out_ref, sv_scr, o_scr, *, tq):
    n_chunks = tq // GMLP_CHUNK
    gc = GMLP_CHUNK
    gwid = GMLP_GROUP_WIDTH

    for g in range(GMLP_GROUPS):
        cols = slice(g * gwid, (g + 1) * gwid)
        vcat = jnp.concatenate([v_ref[c * gc:(c + 1) * gc, cols] for c in range(n_chunks)], axis=1)
        sv = jnp.dot(ws_ref[g], vcat, preferred_element_type=F32)
        for c in range(n_chunks):
            sv_scr[c * gc:(c + 1) * gc, cols] = sv[:, c * gc:(c + 1) * gc]
    for c in range(n_chunks):
        rows = slice(c * gc, (c + 1) * gc)
        sv_scr[rows, :] = (u_ref[rows, :].astype(F32) * (sv_scr[rows, :] + bexp_ref[...])
                           * gate_ref[rows, :].astype(F32))

    nt_dims = (((1,), (1,)), ((), ()))
    scale = XATTN_HEAD_DIM ** -0.5
    for h in range(XATTN_HEADS):
        hc = slice(h * XATTN_HEAD_DIM, (h + 1) * XATTN_HEAD_DIM)
        s = lax.dot_general(xq_ref[:, hc], k_ref[0, :, hc], nt_dims, preferred_element_type=F32) * scale
        e = jnp.exp(s - jnp.max(s, axis=-1, keepdims=True))
        p = e / jnp.sum(e, axis=-1, keepdims=True)
        o = jnp.dot(p.astype(BF16), vmem_ref[0, :, hc], preferred_element_type=F32)
        o_scr[:, hc] = o * xgate_ref[:, hc].astype(F32)

    merged = m0_ref[...].astype(F32) * jnp.dot(yssd_ref[...], wbs_ref[...], preferred_element_type=F32)
    merged = merged + m1_ref[...].astype(F32) * jnp.dot(sv_scr[...].astype(BF16), wbg_ref[...],
                                                        preferred_element_type=F32)
    merged = merged + m2_ref[...].astype(F32) * jnp.dot(o_scr[...].astype(BF16), wbx_ref[...],
                                                        preferred_element_type=F32)
    out = jnp.dot(merged.astype(BF16), wout_ref[...], preferred_element_type=F32)
    out_ref[...] = x_ref[...] + _rms(out, gpost_ref[...])


def _tail(y_ssd, proj, x2, kv, ws, bexp, wbs, wbg, wbx, wout, gpost, *, seq_len, tq=512):
    t = x2.shape[0]
    assert seq_len % tq == 0 and tq % GMLP_CHUNK == 0
    per_seq = seq_len // tq
    resident = functools.partial(pl.BlockSpec, pipeline_mode=pl.Buffered(1))

    def proj_block(blk):
        return pl.BlockSpec((tq, COL_BLOCK), lambda i: (i, blk))

    return pl.pallas_call(
        functools.partial(_tail_kernel, tq=tq),
        grid=(t // tq,),
        in_specs=[
            pl.BlockSpec((tq, SSD_WIDTH), lambda i: (i, 0)),
            proj_block(BLK_GATE), proj_block(BLK_U), proj_block(BLK_V), proj_block(BLK_XQ), proj_block(BLK_XGATE),
            proj_block(BLK_MERGE), proj_block(BLK_MERGE + 1), proj_block(BLK_MERGE + 2),
            pl.BlockSpec((tq, D_MODEL), lambda i: (i, 0)),
            pl.BlockSpec((1, MEM_LEN, D_MODEL), lambda i: (i // per_seq, 0, 0)),
            pl.BlockSpec((1, MEM_LEN, D_MODEL), lambda i: (i // per_seq, 0, 1)),
            resident((GMLP_GROUPS, GMLP_CHUNK, GMLP_CHUNK), lambda i: (0, 0, 0)),
            resident((GMLP_CHUNK, GMLP_WIDTH), lambda i: (0, 0)),
            resident((SSD_WIDTH, D_MODEL), lambda i: (0, 0)),
            resident((GMLP_WIDTH, D_MODEL), lambda i: (0, 0)),
            resident((D_MODEL, D_MODEL), lambda i: (0, 0)),
            resident((D_MODEL, D_MODEL), lambda i: (0, 0)),
            resident((1, D_MODEL), lambda i: (0, 0)),
        ],
        out_specs=pl.BlockSpec((tq, D_MODEL), lambda i: (i, 0)),
        out_shape=jax.ShapeDtypeStruct((t, D_MODEL), F32),
        scratch_shapes=[
            pltpu.VMEM((tq, GMLP_WIDTH), F32),
            pltpu.VMEM((tq, D_MODEL), F32),
        ],
        compiler_params=pltpu.CompilerParams(
            dimension_semantics=("parallel",),
            vmem_limit_bytes=56 * 1024 * 1024),
        name="tail",
    )(y_ssd, proj, proj, proj, proj, proj, proj, proj, proj, x2, kv, kv,
      ws, bexp, wbs, wbg, wbx, wout, gpost)


def _split_bf16(w):
    hi = w.astype(BF16)
    return hi, (w - hi.astype(F32)).astype(BF16)


def _prep_layer(p, l):
    w_in = p['w_in'][l]
    dt0 = SSD_WIDTH + SSD_WIDTH + 2 * SSD_GROUPS * SSD_STATE
    dt1 = dt0 + 2 * SSD_HEADS
    w_main = jnp.concatenate([w_in[:, :dt0], w_in[:, dt1:]], axis=1).astype(BF16)
    w_dt = w_in[:, dt0:dt1].reshape(D_MODEL, 2, SSD_GROUPS, HEADS_PER_GROUP)
    w_dt = w_dt.transpose(2, 1, 3, 0).reshape(2 * SSD_HEADS, D_MODEL)
    wdt_hi, wdt_lo = _split_bf16(w_dt)

    def per_group_rows(v):
        v = v.reshape(2, SSD_GROUPS, HEADS_PER_GROUP).transpose(1, 0, 2).reshape(SSD_GROUPS, DT_ROWS, 1)
        return jnp.broadcast_to(v, (SSD_GROUPS, DT_ROWS, LANES))

    return dict(
        dtb=per_group_rows(p['dt_bias'][l]), alog=per_group_rows(p['a_log'][l]),
        dskip=jnp.repeat(p['d_skip'][l], SSD_HEAD_DIM).reshape(1, SSD_WIDTH),
        ssd_ng=p['ssd_norm_g'][l].reshape(1, SSD_WIDTH),
        gpre=p['norm_pre_g'][l].reshape(1, D_MODEL),
        w_main=w_main, wdt_hi=wdt_hi, wdt_lo=wdt_lo,
        convw=p['conv_w'][l], convb=p['conv_b'][l].reshape(1, -1),
        lng=p['gmlp_ln_g'][l].reshape(1, -1), lnb=p['gmlp_ln_b'][l].reshape(1, -1),
        ws=p['w_spatial'][l].astype(BF16),
        bexp=jnp.repeat(p['b_spatial'][l].T, GMLP_GROUP_WIDTH, axis=1),
        mem_g=p['mem_norm_g'][l].reshape(1, D_MODEL), w_kv=p['w_kv'][l].astype(BF16),
        wbs=p['w_br_ssd'][l].astype(BF16), wbg=p['w_br_gmlp'][l].astype(BF16),
        wbx=p['w_br_xattn'][l].astype(BF16), wout=p['w_out'][l].astype(BF16),
        gpost=p['norm_post_g'][l].reshape(1, D_MODEL),
    )


def _layer(x2, mem, prm, *, batch, seq_len):
    proj, dt4 = _inproj(x2, prm['gpre'], prm['w_main'], prm['wdt_hi'], prm['wdt_lo'],
                        prm['convw'], prm['convb'], prm['lng'], prm['lnb'], seq_len=seq_len)
    y_ssd = _ssd(proj, dt4, prm['dtb'], prm['alog'], prm['dskip'], prm['ssd_ng'], batch=batch, seq_len=seq_len)
    kv = _memkv(mem, prm['mem_g'], prm['w_kv'])
    return _tail(y_ssd, proj, x2, kv, prm['ws'], prm['bexp'], prm['wbs'], prm['wbg'], prm['wbx'], prm['wout'],
                 prm['gpost'], seq_len=seq_len)


def kernel(x, mem, norm_pre_g, w_in, conv_w, conv_b, dt_bias, a_log, d_skip, ssd_norm_g, gmlp_ln_g, gmlp_ln_b, w_spatial, b_spatial, mem_norm_g, w_kv, w_br_ssd, w_br_gmlp, w_br_xattn, w_out, norm_post_g):
    p = dict(norm_pre_g=norm_pre_g, w_in=w_in, conv_w=conv_w, conv_b=conv_b, dt_bias=dt_bias, a_log=a_log,
             d_skip=d_skip, ssd_norm_g=ssd_norm_g, gmlp_ln_g=gmlp_ln_g, gmlp_ln_b=gmlp_ln_b, w_spatial=w_spatial,
             b_spatial=b_spatial, mem_norm_g=mem_norm_g, w_kv=w_kv, w_br_ssd=w_br_ssd, w_br_gmlp=w_br_gmlp,
             w_br_xattn=w_br_xattn, w_out=w_out, norm_post_g=norm_post_g)
    batch, seq_len, _ = x.shape
    x2 = x.reshape(batch * seq_len, D_MODEL)
    for l in range(w_in.shape[0]):
        x2 = _layer(x2, mem, _prep_layer(p, l), batch=batch, seq_len=seq_len)
    return x2.reshape(batch, seq_len, D_MODEL)
```

```python
import functools

import jax
import jax.numpy as jnp
from jax import lax
from jax.experimental import pallas as pl
from jax.experimental.pallas import tpu as pltpu

F32 = jnp.float32
BF16 = jnp.bfloat16

EPS = 1e-6
D_MODEL = 1024
N_BRANCH = 3

SSD_WIDTH = 2 * D_MODEL
SSD_HEAD_DIM = 64
SSD_HEADS = SSD_WIDTH // SSD_HEAD_DIM
SSD_GROUPS = 8
SSD_STATE = 128
SSD_CONV = 5
SSD_CHUNK = 128
HEADS_PER_GROUP = SSD_HEADS // SSD_GROUPS
GROUP_WIDTH = HEADS_PER_GROUP * SSD_HEAD_DIM
DT_ROWS = 2 * HEADS_PER_GROUP

GMLP_WIDTH = D_MODEL
GMLP_GROUPS = 8
GMLP_CHUNK = 128
GMLP_GROUP_WIDTH = GMLP_WIDTH // GMLP_GROUPS

XATTN_HEADS = 4
XATTN_HEAD_DIM = D_MODEL // XATTN_HEADS
MEM_LEN = 256

LANES = 128
BF16_ROWS = 16

COL_BLOCK = 1024
BLK_Z, BLK_XS, BLK_BM, BLK_CM, BLK_GATE, BLK_U, BLK_V, BLK_XQ, BLK_XGATE, BLK_MERGE = 0, 2, 4, 5, 6, 7, 8, 9, 10, 11
N_COL_BLOCKS = 14
PROJ_COLS = N_COL_BLOCKS * COL_BLOCK
HALO = BF16_ROWS
CONV_PAD = SSD_CONV // 2


def _silu(x):
    return x * jax.nn.sigmoid(x)


def _gelu_tanh(x):
    c = 0.7978845608028654
    return x * (0.5 * (1.0 + jnp.tanh(c * (x + 0.044715 * (x * x * x)))))


def _rms(x, g):
    return x * lax.rsqrt(jnp.mean(x * x, axis=-1, keepdims=True) + EPS) * g


def _inproj_kernel(xprev_ref, x_ref, xnext_ref, gpre_ref, w_ref, wdth_ref, wdtl_ref,
                   convw_ref, convb_ref, lng_ref, lnb_ref,
                   out_ref, dt_ref, h_scr, h32_scr, hperm_scr, ynat_scr, *, tm, sub, tiles_per_seq):
    i = pl.program_id(0)
    j = pl.program_id(1)
    n_sub = tm // sub
    n_lb = D_MODEL // LANES
    ns = (sub + 2 * HALO) // 8
    nt_dims = (((1,), (1,)), ((), ()))

    def put_h32(row0, h):
        for lb in range(n_lb):
            h32_scr[lb, row0:row0 + h.shape[0], :] = h[:, lb * LANES:(lb + 1) * LANES]

    @pl.when(j == 0)
    def _():
        g = gpre_ref[...]
        first = (i % tiles_per_seq) == 0
        last = (i % tiles_per_seq) == tiles_per_seq - 1
        put_h32(0, _rms(xprev_ref[...], g) * jnp.where(first, 0.0, 1.0))
        put_h32(HALO + tm, _rms(xnext_ref[...], g) * jnp.where(last, 0.0, 1.0))
        for s in range(n_sub):
            h = _rms(x_ref[s * sub:(s + 1) * sub, :], g)
            hb = h.astype(BF16)
            h_scr[s * sub:(s + 1) * sub, :] = hb
            put_h32(HALO + s * sub, h)
            hl = (h - hb.astype(F32)).astype(BF16)
            dt = (lax.dot_general(wdth_ref[...], hb, nt_dims, preferred_element_type=F32)
                  + lax.dot_general(wdth_ref[...], hl, nt_dims, preferred_element_type=F32)
                  + lax.dot_general(wdtl_ref[...], hb, nt_dims, preferred_element_type=F32))
            for c in range(sub // SSD_CHUNK):
                dt_ref[s * (sub // SSD_CHUNK) + c] = dt[:, c * SSD_CHUNK:(c + 1) * SSD_CHUNK]
        for s in range(n_sub):
            for b in range(0, ns, 2):
                slabs = [jnp.concatenate([h32_scr[lb, pl.ds(s * sub + b + d, 8, stride=ns), :]
                                          for lb in range(n_lb)], axis=1) for d in range(2)]
                hperm_scr[s, 8 * b:8 * b + BF16_ROWS, :] = jnp.concatenate(slabs, axis=0).astype(BF16)

    def plain(act):
        for s in range(n_sub):
            rows = h_scr[s * sub:(s + 1) * sub, :]
            acc = jnp.dot(rows, w_ref[...], preferred_element_type=F32)
            out_ref[s * sub:(s + 1) * sub, :] = act(acc).astype(BF16)

    is_silu = (j == BLK_Z) | (j == BLK_Z + 1) | (j == BLK_GATE) | (j == BLK_XGATE)
    is_conv = (j >= BLK_XS) & (j <= BLK_CM)
    is_sig = j >= BLK_MERGE

    @pl.when(is_silu)
    def _():
        plain(_silu)

    @pl.when(j == BLK_XQ)
    def _():
        plain(lambda a: a)

    @pl.when(j == BLK_U)
    def _():
        plain(_gelu_tanh)

    @pl.when(j == BLK_V)
    def _():
        def gelu_ln(a):
            v = _gelu_tanh(a)
            mu = jnp.mean(v, axis=-1, keepdims=True)
            vc = v - mu
            return vc * lax.rsqrt(jnp.mean(vc * vc, axis=-1, keepdims=True) + EPS) * lng_ref[...] + lnb_ref[...]
        plain(gelu_ln)

    @pl.when(is_sig)
    def _():
        plain(jax.nn.sigmoid)

    @pl.when(is_conv)
    def _():
        taps = [convw_ref[k:k + 1, :].reshape(1, 1, COL_BLOCK) for k in range(SSD_CONV)]
        project = lambda s: jnp.dot(hperm_scr[s], w_ref[...], preferred_element_type=F32).reshape(ns, 8, COL_BLOCK)
        nxt = project(0)
        for s in range(n_sub):
            acc = nxt
            if s + 1 < n_sub:
                nxt = project(s + 1)
            ext = jnp.concatenate([pltpu.roll(acc[ns - CONV_PAD:ns], 1, 1), acc,
                                   pltpu.roll(acc[0:CONV_PAD], 8 - 1, 1)], axis=0)
            y = convb_ref[...].reshape(1, 1, COL_BLOCK)
            for k in range(SSD_CONV):
                y = y + taps[k] * ext[k:k + ns]
            for b in range(ns):
                for lb in range(n_lb):
                    ynat_scr[s % 2, lb, pl.ds(b, 8, stride=ns), :] = y[b][:, lb * LANES:(lb + 1) * LANES]
            y = jnp.concatenate([ynat_scr[s % 2, lb, HALO:HALO + sub, :] for lb in range(n_lb)], axis=1)
            out_ref[s * sub:(s + 1) * sub, :] = _silu(y).astype(BF16)


def _inproj(x2, gpre, w_main, wdt_hi, wdt_lo, convw, convb, lng, lnb, *, seq_len, tm=2048, sub=512):
    t = x2.shape[0]
    assert t % tm == 0 and seq_len % tm == 0 and tm % sub == 0 and sub % SSD_CHUNK == 0
    win = sub + 2 * HALO
    assert win % 8 == 0 and (win // 8) % 8 != 0
    n_tiles = t // tm
    halo_blocks = t // HALO
    per_tile = tm // HALO
    kernel = functools.partial(_inproj_kernel, tm=tm, sub=sub, tiles_per_seq=seq_len // tm)
    conv_idx = lambda i, j: (0, jnp.clip(j - BLK_XS, 0, BLK_CM - BLK_XS))
    return pl.pallas_call(
        kernel,
        grid=(n_tiles, N_COL_BLOCKS),
        in_specs=[
            pl.BlockSpec((HALO, D_MODEL), lambda i, j: (jnp.maximum(i * per_tile - 1, 0), 0)),
            pl.BlockSpec((tm, D_MODEL), lambda i, j: (i, 0)),
            pl.BlockSpec((HALO, D_MODEL), lambda i, j: (jnp.minimum((i + 1) * per_tile, halo_blocks - 1), 0)),
            pl.BlockSpec((1, D_MODEL), lambda i, j: (0, 0)),
            pl.BlockSpec((D_MODEL, COL_BLOCK), lambda i, j: (0, j)),
            pl.BlockSpec((2 * SSD_HEADS, D_MODEL), lambda i, j: (0, 0)),
            pl.BlockSpec((2 * SSD_HEADS, D_MODEL), lambda i, j: (0, 0)),
            pl.BlockSpec((SSD_CONV, COL_BLOCK), conv_idx),
            pl.BlockSpec((1, COL_BLOCK), conv_idx),
            pl.BlockSpec((1, COL_BLOCK), lambda i, j: (0, 0)),
            pl.BlockSpec((1, COL_BLOCK), lambda i, j: (0, 0)),
        ],
        out_specs=[
            pl.BlockSpec((tm, COL_BLOCK), lambda i, j: (i, j)),
            pl.BlockSpec((tm // SSD_CHUNK, 2 * SSD_HEADS, SSD_CHUNK), lambda i, j: (i, 0, 0)),
        ],
        out_shape=[
            jax.ShapeDtypeStruct((t, PROJ_COLS), BF16),
            jax.ShapeDtypeStruct((t // SSD_CHUNK, 2 * SSD_HEADS, SSD_CHUNK), F32),
        ],
        scratch_shapes=[
            pltpu.VMEM((tm, D_MODEL), BF16),
            pltpu.VMEM((D_MODEL // LANES, tm + 2 * HALO, LANES), F32),
            pltpu.VMEM((tm // sub, win, D_MODEL), BF16),
            pltpu.VMEM((2, COL_BLOCK // LANES, win, LANES), F32),
        ],
        compiler_params=pltpu.CompilerParams(
            dimension_semantics=("parallel", "arbitrary"),
            vmem_limit_bytes=58 * 1024 * 1024),
        name="inproj",
    )(x2, x2, x2, gpre, w_main, wdt_hi, wdt_lo, convw, convb, lng, lnb)


TRI_BLOCKS = 5
EXPAND_BLOCKS = 4
COL_CS, COL_ECS, COL_W = 0, DT_ROWS, 2 * DT_ROWS


def _softplus(x):
    return jnp.maximum(x, 0.0) + jnp.log1p(jnp.exp(-jnp.abs(x)))


def _dt_rows(dt_raw, dt_bias, a, tri):
    rows, q = dt_raw.shape
    dt = _softplus(dt_raw + dt_bias)
    da = dt * a
    hi = da.astype(BF16).astype(F32)
    r1 = da - hi
    mid = r1.astype(BF16).astype(F32)
    lo = r1 - mid
    parts = [hi, mid, lo]
    if (3 * rows) % BF16_ROWS:
        parts.append(jnp.zeros((BF16_ROWS - (3 * rows) % BF16_ROWS, q), F32))
    sums = jnp.dot(jnp.concatenate(parts, axis=0).astype(BF16), tri, preferred_element_type=F32)
    s = sums[0:rows] + sums[rows:2 * rows] + sums[2 * rows:3 * rows]
    fwd = lax.broadcasted_iota(jnp.int32, (rows, q), 0) % DT_ROWS < HEADS_PER_GROUP
    cs = jnp.where(fwd, s[:, 0:q], s[:, q:2 * q])
    rem = jnp.where(fwd, s[:, 2 * q:3 * q], s[:, 3 * q:4 * q])
    total = s[:, 4 * q:5 * q]
    return dt, cs, jnp.exp(cs), dt * jnp.exp(rem), jnp.exp(total)


def _token_columns(cs, ecs, w):
    pad = jnp.zeros((SSD_CHUNK - 3 * DT_ROWS, SSD_CHUNK), F32)
    return jnp.transpose(jnp.concatenate([cs, ecs, w, pad], axis=0))


def _head_row(v, d):
    lane = lax.broadcasted_iota(jnp.int32, (1, LANES), 1)
    r = d * HEADS_PER_GROUP
    lo = jnp.where(lane < SSD_HEAD_DIM, v[r:r + 1], v[r + 1:r + 2])
    hi = jnp.where(lane < SSD_HEAD_DIM, v[r + 2:r + 3], v[r + 3:r + 4])
    return jnp.concatenate([lo, hi], axis=1)


def _ssd_kernel(xs0_ref, bm0_ref, dt0_ref, dtb0_ref, alog0_ref,
                xs_ref, bm_ref, cm_ref, z_ref, dt_ref, dtb_ref, alog_ref, dskip_ref, ng_ref,
                tri_ref, exp_ref, y_ref, hb_all, hf_scr, hb_scr, *, cpb):
    k = pl.program_id(0)
    i = pl.program_id(1)
    nb = pl.num_programs(1)
    q = SSD_CHUNK
    gw = GROUP_WIDTH
    nr = cpb * DT_ROWS
    chunks = range(cpb)
    rows_of = lambda c: slice(c * q, (c + 1) * q)
    heads_of = lambda v, c: v[c * DT_ROWS:(c + 1) * DT_ROWS]
    wslot = k % 2
    rslot = 1 - wslot

    @pl.when((k == 0) & (i == 0))
    def _():
        hb_all[1] = jnp.zeros(hb_all.shape[1:], hb_all.dtype)

    @pl.when(i == 0)
    def _():
        hb_scr[...] = jnp.zeros_like(hb_scr)
        hf_scr[...] = jnp.zeros_like(hf_scr)

    nt_dims = (((1,), (1,)), ((), ()))
    scores = [lax.dot_general(cm_ref[rows_of(c), :], bm_ref[rows_of(c), :], nt_dims,
                              preferred_element_type=F32) for c in chunks]
    tile = lambda v: jnp.concatenate([v] * cpb, axis=0)
    dt_raw = jnp.concatenate([dt0_ref[...].reshape(nr, q), dt_ref[...].reshape(nr, q)], axis=0)
    bias = jnp.concatenate([tile(dtb0_ref[0]), tile(dtb_ref[0])], axis=0)
    a = jnp.concatenate([tile(-jnp.exp(alog0_ref[0])), tile(-jnp.exp(alog_ref[0]))], axis=0)
    dt, cs, ecs, w, cd = _dt_rows(dt_raw, bias, a, tri_ref[...])
    cols = [_token_columns(heads_of(cs, c), heads_of(ecs, c), heads_of(w, c)) for c in range(2 * cpb)]
    cols0, cols = cols[:cpb], cols[cpb:]
    cd0 = cd[:nr]
    dt, cs, cd = (v[nr:] for v in (dt, cs, cd))

    wexp0 = jnp.dot(jnp.concatenate(cols0, axis=0).astype(BF16), exp_ref[:, 3 * gw:4 * gw],
                    preferred_element_type=F32)
    expd = jnp.dot(jnp.concatenate(cols, axis=0).astype(BF16), exp_ref[:, 0:3 * gw],
                   preferred_element_type=F32)

    def local_state(bm, xs, wexp):
        bmt = jnp.transpose(bm.astype(F32)).astype(BF16)
        return jnp.dot(bmt, (xs.astype(F32) * wexp).astype(BF16), preferred_element_type=F32)

    st0 = [local_state(bm0_ref[rows_of(c), :], xs0_ref[rows_of(c), :], wexp0[rows_of(c)]) for c in chunks]
    h = hb_scr[...]
    for c in reversed(chunks):
        hb_all[wslot, (nb - 1 - i) * cpb + c] = h.astype(BF16)
        h = h * _head_row(heads_of(cd0, c), 1) + st0[c]
    hb_scr[...] = h

    t_idx = lax.broadcasted_iota(jnp.int32, (q, q), 0)
    s_idx = lax.broadcasted_iota(jnp.int32, (q, q), 1)
    lower = s_idx < t_idx
    diag = s_idx == t_idx
    lane_head = lax.broadcasted_iota(jnp.int32, (q, gw), 1) // SSD_HEAD_DIM

    def intra_chunk(c, sc):
        dt_c, cs_c = heads_of(dt, c), heads_of(cs, c)
        xs_b = xs_ref[rows_of(c), :]
        cols_c = cols[c]
        y = jnp.zeros((q, gw), F32)
        for r in range(HEADS_PER_GROUP):
            rb = HEADS_PER_GROUP + r
            colf = jnp.broadcast_to(cols_c[:, COL_CS + r:COL_CS + r + 1], (q, q))
            colb = jnp.broadcast_to(cols_c[:, COL_CS + rb:COL_CS + rb + 1], (q, q))
            arg = jnp.where(lower, colf - cs_c[r:r + 1], colb - cs_c[rb:rb + 1])
            dsel = jnp.where(lower, dt_c[r:r + 1],
                             jnp.where(diag, dt_c[r:r + 1] + dt_c[rb:rb + 1], dt_c[rb:rb + 1]))
            m = (sc * jnp.exp(arg) * dsel).astype(BF16)
            x_r = jnp.where(lane_head == r, xs_b, jnp.zeros_like(xs_b))
            y = y + jnp.dot(m, x_r, preferred_element_type=F32)
        return y

    def finish(c, y, yoff):
        ex = expd[rows_of(c)]
        y = y + yoff[:, 0:gw] * ex[:, 0:gw] + yoff[:, gw:2 * gw] * ex[:, gw:2 * gw]
        y = y + dskip_ref[...] * xs_ref[rows_of(c), :].astype(F32)
        y = y * z_ref[rows_of(c), :].astype(F32)
        y = y * lax.rsqrt(jnp.mean(y * y, axis=-1, keepdims=True) + EPS) * ng_ref[...]
        y_ref[rows_of(c), :] = y.astype(BF16)

    st = [local_state(bm_ref[rows_of(c), :], xs_ref[rows_of(c), :], expd[rows_of(c), 2 * gw:3 * gw])
          for c in chunks]
    h = hf_scr[...]
    yoff = []
    for c in chunks:
        hcat = jnp.concatenate([h.astype(BF16), hb_all[rslot, i * cpb + c]], axis=1)
        yoff.append(jnp.dot(cm_ref[rows_of(c), :], hcat, preferred_element_type=F32))
        h = h * _head_row(heads_of(cd, c), 0) + st[c]
    hf_scr[...] = h
    for c in chunks:
        finish(c, intra_chunk(c, scores[c]), yoff[c])


def _ssd_constants():
    q = SSD_CHUNK
    u = jnp.arange(q)[:, None]
    s = jnp.arange(q)[None, :]
    tri = jnp.concatenate([u <= s, u >= s, u > s, u < s, jnp.ones((q, q), bool)], axis=1).astype(BF16)
    row = jnp.arange(q)[:, None]
    col = jnp.arange(EXPAND_BLOCKS * GROUP_WIDTH)[None, :]
    src = jnp.array([COL_ECS, COL_ECS + HEADS_PER_GROUP, COL_W, COL_W + HEADS_PER_GROUP])[col // GROUP_WIDTH]
    expand = (row == src + (col % GROUP_WIDTH) // SSD_HEAD_DIM).astype(BF16)
    return tri, expand


def _ssd(proj, dt4, dtb, alog, dskip, ng, *, batch, seq_len, tq=1024):
    t = proj.shape[0]
    assert seq_len % tq == 0 and tq % SSD_CHUNK == 0
    nb = seq_len // tq
    cpb = tq // SSD_CHUNK
    tri, expand = _ssd_constants()
    gw = GROUP_WIDTH
    xs0, bm0, cm0 = BLK_XS * COL_BLOCK // gw, BLK_BM * COL_BLOCK // SSD_STATE, BLK_CM * COL_BLOCK // SSD_STATE

    n_pairs = batch * SSD_GROUPS

    def back(k):
        p = jnp.minimum(k, n_pairs - 1)
        return p // SSD_GROUPS, p % SSD_GROUPS

    def out(k):
        p = jnp.maximum(k - 1, 0)
        return p // SSD_GROUPS, p % SSD_GROUPS

    def back_rows(k, i):
        return back(k)[0] * nb + nb - 1 - i

    def out_rows(k, i):
        return out(k)[0] * nb + i

    def dt_specs(pair):
        rows = back_rows if pair is back else out_rows
        grp = (1, DT_ROWS, LANES)
        return [
            pl.BlockSpec((cpb, DT_ROWS, SSD_CHUNK), lambda k, i: (rows(k, i), pair(k)[1], 0)),
            pl.BlockSpec(grp, lambda k, i: (pair(k)[1], 0, 0)),
            pl.BlockSpec(grp, lambda k, i: (pair(k)[1], 0, 0)),
        ]

    const2 = lambda k, i: (0, 0)
    return pl.pallas_call(
        functools.partial(_ssd_kernel, cpb=cpb),
        grid=(n_pairs + 1, nb),
        in_specs=[
            pl.BlockSpec((tq, gw), lambda k, i: (back_rows(k, i), xs0 + back(k)[1])),
            pl.BlockSpec((tq, SSD_STATE), lambda k, i: (back_rows(k, i), bm0 + back(k)[1])),
            *dt_specs(back),
            pl.BlockSpec((tq, gw), lambda k, i: (out_rows(k, i), xs0 + out(k)[1])),
            pl.BlockSpec((tq, SSD_STATE), lambda k, i: (out_rows(k, i), bm0 + out(k)[1])),
            pl.BlockSpec((tq, SSD_STATE), lambda k, i: (out_rows(k, i), cm0 + out(k)[1])),
            pl.BlockSpec((tq, gw), lambda k, i: (out_rows(k, i), out(k)[1])),
            *dt_specs(out),
            pl.BlockSpec((1, gw), lambda k, i: (0, out(k)[1])),
            pl.BlockSpec((1, gw), lambda k, i: (0, out(k)[1])),
            pl.BlockSpec((SSD_CHUNK, TRI_BLOCKS * SSD_CHUNK), const2),
            pl.BlockSpec((SSD_CHUNK, EXPAND_BLOCKS * gw), const2),
        ],
        out_specs=pl.BlockSpec((tq, gw), lambda k, i: (jnp.where(k == 0, t // tq, out_rows(k, i)),
                                                       jnp.where(k == 0, i % SSD_GROUPS, out(k)[1]))),
        out_shape=jax.ShapeDtypeStruct((t + tq, SSD_WIDTH), BF16),
        scratch_shapes=[
            pltpu.VMEM((2, seq_len // SSD_CHUNK, SSD_STATE, gw), BF16),
            pltpu.VMEM((SSD_STATE, gw), F32),
            pltpu.VMEM((SSD_STATE, gw), F32),
        ],
        compiler_params=pltpu.CompilerParams(
            dimension_semantics=("arbitrary", "arbitrary"),
            vmem_limit_bytes=48 * 1024 * 1024),
        name="ssd",
    )(proj, proj, dt4, dtb, alog, proj, proj, proj, proj, dt4, dtb, alog, dskip, ng, tri, expand)


def _memkv_kernel(mem_ref, g_ref, w_ref, kv_ref):
    m = _rms(mem_ref[0], g_ref[...]).astype(BF16)
    kv_ref[0] = jnp.dot(m, w_ref[...], preferred_element_type=F32).astype(BF16)


def _memkv(mem, g, w_kv):
    b = mem.shape[0]
    return pl.pallas_call(
        _memkv_kernel,
        grid=(b,),
        in_specs=[
            pl.BlockSpec((1, MEM_LEN, D_MODEL), lambda i: (i, 0, 0)),
            pl.BlockSpec((1, D_MODEL), lambda i: (0, 0)),
            pl.BlockSpec((D_MODEL, 2 * D_MODEL), lambda i: (0, 0)),
        ],
        out_specs=pl.BlockSpec((1, MEM_LEN, 2 * D_MODEL), lambda i: (i, 0, 0)),
        out_shape=jax.ShapeDtypeStruct((b, MEM_LEN, 2 * D_MODEL), BF16),
        compiler_params=pltpu.CompilerParams(dimension_semantics=("parallel",)),
        name="memkv",
    )(mem, g, w_kv)


def _tail_kernel(yssd_ref, gate_ref, u_ref, v_ref, xq_ref, xgate_ref, m0_ref, m1_ref, m2_ref, x_ref,
                 k_ref, vmem_ref, ws_ref, bexp_ref, wbs_ref, wbg_ref, wbx_ref, wout_ref, gpost_ref,
                 out_ref, sv_scr, o_scr, *, tq):
    n_chunks = tq // GMLP_CHUNK
    gc = GMLP_CHUNK
    gwid = GMLP_GROUP_WIDTH

    for g in range(GMLP_GROUPS):
        cols = slice(g * gwid, (g + 1) * gwid)
        vcat = jnp.concatenate([v_ref[c * gc:(c + 1) * gc, cols] for c in range(n_chunks)], axis=1)
        sv = jnp.dot(ws_ref[g], vcat, preferred_element_type=F32)
        for c in range(n_chunks):
            sv_scr[c * gc:(c + 1) * gc, cols] = sv[:, c * gc:(c + 1) * gc]
    for c in range(n_chunks):
        rows = slice(c * gc, (c + 1) * gc)
        sv_scr[rows, :] = (u_ref[rows, :].astype(F32) * (sv_scr[rows, :] + bexp_ref[...])
                           * gate_ref[rows, :].astype(F32))

    nt_dims = (((1,), (1,)), ((), ()))
    scale = XATTN_HEAD_DIM ** -0.5
    for h in range(XATTN_HEADS):
        hc = slice(h * XATTN_HEAD_DIM, (h + 1) * XATTN_HEAD_DIM)
        s = lax.dot_general(xq_ref[:, hc], k_ref[0, :, hc], nt_dims, preferred_element_type=F32) * scale
        e = jnp.exp(s - jnp.max(s, axis=-1, keepdims=True))
        p = e / jnp.sum(e, axis=-1, keepdims=True)
        o = jnp.dot(p.astype(BF16), vmem_ref[0, :, hc], preferred_element_type=F32)
        o_scr[:, hc] = o * xgate_ref[:, hc].astype(F32)

    merged = m0_ref[...].astype(F32) * jnp.dot(yssd_ref[...], wbs_ref[...], preferred_element_type=F32)
    merged = merged + m1_ref[...].astype(F32) * jnp.dot(sv_scr[...].astype(BF16), wbg_ref[...],
                                                        preferred_element_type=F32)
    merged = merged + m2_ref[...].astype(F32) * jnp.dot(o_scr[...].astype(BF16), wbx_ref[...],
                                                        preferred_element_type=F32)
    out = jnp.dot(merged.astype(BF16), wout_ref[...], preferred_element_type=F32)
    out_ref[...] = x_ref[...] + _rms(out, gpost_ref[...])


def _tail(y_ssd, proj, x2, kv, ws, bexp, wbs, wbg, wbx, wout, gpost, *, seq_len, tq=512):
    t = x2.shape[0]
    assert seq_len % tq == 0 and tq % GMLP_CHUNK == 0
    per_seq = seq_len // tq
    resident = functools.partial(pl.BlockSpec, pipeline_mode=pl.Buffered(1))

    def proj_block(blk):
        return pl.BlockSpec((tq, COL_BLOCK), lambda i: (i, blk))

    return pl.pallas_call(
        functools.partial(_tail_kernel, tq=tq),
        grid=(t // tq,),
        in_specs=[
            pl.BlockSpec((tq, SSD_WIDTH), lambda i: (i, 0)),
            proj_block(BLK_GATE), proj_block(BLK_U), proj_block(BLK_V), proj_block(BLK_XQ), proj_block(BLK_XGATE),
            proj_block(BLK_MERGE), proj_block(BLK_MERGE + 1), proj_block(BLK_MERGE + 2),
            pl.BlockSpec((tq, D_MODEL), lambda i: (i, 0)),
            pl.BlockSpec((1, MEM_LEN, D_MODEL), lambda i: (i // per_seq, 0, 0)),
            pl.BlockSpec((1, MEM_LEN, D_MODEL), lambda i: (i // per_seq, 0, 1)),
            resident((GMLP_GROUPS, GMLP_CHUNK, GMLP_CHUNK), lambda i: (0, 0, 0)),
            resident((GMLP_CHUNK, GMLP_WIDTH), lambda i: (0, 0)),
            resident((SSD_WIDTH, D_MODEL), lambda i: (0, 0)),
            resident((GMLP_WIDTH, D_MODEL), lambda i: (0, 0)),
            resident((D_MODEL, D_MODEL), lambda i: (0, 0)),
            resident((D_MODEL, D_MODEL), lambda i: (0, 0)),
            resident((1, D_MODEL), lambda i: (0, 0)),
        ],
        out_specs=pl.BlockSpec((tq, D_MODEL), lambda i: (i, 0)),
        out_shape=jax.ShapeDtypeStruct((t, D_MODEL), F32),
        scratch_shapes=[
            pltpu.VMEM((tq, GMLP_WIDTH), F32),
            pltpu.VMEM((tq, D_MODEL), F32),
        ],
        compiler_params=pltpu.CompilerParams(
            dimension_semantics=("parallel",),
            vmem_limit_bytes=56 * 1024 * 1024),
        name="tail",
    )(y_ssd, proj, proj, proj, proj, proj, proj, proj, proj, x2, kv, kv,
      ws, bexp, wbs, wbg, wbx, wout, gpost)


def _split_bf16(w):
    hi = w.astype(BF16)
    return hi, (w - hi.astype(F32)).astype(BF16)


def _prep_layer(p, l):
    w_in = p['w_in'][l]
    dt0 = SSD_WIDTH + SSD_WIDTH + 2 * SSD_GROUPS * SSD_STATE
    dt1 = dt0 + 2 * SSD_HEADS
    w_main = jnp.concatenate([w_in[:, :dt0], w_in[:, dt1:]], axis=1).astype(BF16)
    w_dt = w_in[:, dt0:dt1].reshape(D_MODEL, 2, SSD_GROUPS, HEADS_PER_GROUP)
    w_dt = w_dt.transpose(2, 1, 3, 0).reshape(2 * SSD_HEADS, D_MODEL)
    wdt_hi, wdt_lo = _split_bf16(w_dt)

    def per_group_rows(v):
        v = v.reshape(2, SSD_GROUPS, HEADS_PER_GROUP).transpose(1, 0, 2).reshape(SSD_GROUPS, DT_ROWS, 1)
        return jnp.broadcast_to(v, (SSD_GROUPS, DT_ROWS, LANES))

    return dict(
        dtb=per_group_rows(p['dt_bias'][l]), alog=per_group_rows(p['a_log'][l]),
        dskip=jnp.repeat(p['d_skip'][l], SSD_HEAD_DIM).reshape(1, SSD_WIDTH),
        ssd_ng=p['ssd_norm_g'][l].reshape(1, SSD_WIDTH),
        gpre=p['norm_pre_g'][l].reshape(1, D_MODEL),
        w_main=w_main, wdt_hi=wdt_hi, wdt_lo=wdt_lo,
        convw=p['conv_w'][l], convb=p['conv_b'][l].reshape(1, -1),
        lng=p['gmlp_ln_g'][l].reshape(1, -1), lnb=p['gmlp_ln_b'][l].reshape(1, -1),
        ws=p['w_spatial'][l].astype(BF16),
        bexp=jnp.repeat(p['b_spatial'][l].T, GMLP_GROUP_WIDTH, axis=1),
        mem_g=p['mem_norm_g'][l].reshape(1, D_MODEL), w_kv=p['w_kv'][l].astype(BF16),
        wbs=p['w_br_ssd'][l].astype(BF16), wbg=p['w_br_gmlp'][l].astype(BF16),
        wbx=p['w_br_xattn'][l].astype(BF16), wout=p['w_out'][l].astype(BF16),
        gpost=p['norm_post_g'][l].reshape(1, D_MODEL),
    )


def _layer(x2, mem, prm, *, batch, seq_len):
    proj, dt4 = _inproj(x2, prm['gpre'], prm['w_main'], prm['wdt_hi'], prm['wdt_lo'],
                        prm['convw'], prm['convb'], prm['lng'], prm['lnb'], seq_len=seq_len)
    y_ssd = _ssd(proj, dt4, prm['dtb'], prm['alog'], prm['dskip'], prm['ssd_ng'], batch=batch, seq_len=seq_len)
    kv = _memkv(mem, prm['mem_g'], prm['w_kv'])
    return _tail(y_ssd, proj, x2, kv, prm['ws'], prm['bexp'], prm['wbs'], prm['wbg'], prm['wbx'], prm['wout'],
                 prm['gpost'], seq_len=seq_len)


def kernel(x, mem, norm_pre_g, w_in, conv_w, conv_b, dt_bias, a_log, d_skip, ssd_norm_g, gmlp_ln_g, gmlp_ln_b, w_spatial, b_spatial, mem_norm_g, w_kv, w_br_ssd, w_br_gmlp, w_br_xattn, w_out, norm_post_g):
    p = dict(norm_pre_g=norm_pre_g, w_in=w_in, conv_w=conv_w, conv_b=conv_b, dt_bias=dt_bias, a_log=a_log,
             d_skip=d_skip, ssd_norm_g=ssd_norm_g, gmlp_ln_g=gmlp_ln_g, gmlp_ln_b=gmlp_ln_b, w_spatial=w_spatial,
             b_spatial=b_spatial, mem_norm_g=mem_norm_g, w_kv=w_kv, w_br_ssd=w_br_ssd, w_br_gmlp=w_br_gmlp,
             w_br_xattn=w_br_xattn, w_out=w_out, norm_post_g=norm_post_g)
    batch, seq_len, _ = x.shape
    x2 = x.reshape(batch * seq_len, D_MODEL)
    for l in range(w_in.shape[0]):
        x2 = _layer(x2, mem, _prep_layer(p, l), batch=batch, seq_len=seq_len)
    return x2.reshape(batch, seq_len, D_MODEL)
```

```python
import functools

import jax
import jax.numpy as jnp
from jax import lax
from jax.experimental import pallas as pl
from jax.experimental.pallas import tpu as pltpu

F32 = jnp.float32
BF16 = jnp.bfloat16

EPS = 1e-6
D_MODEL = 1024
N_BRANCH = 3

SSD_WIDTH = 2 * D_MODEL
SSD_HEAD_DIM = 64
SSD_HEADS = SSD_WIDTH // SSD_HEAD_DIM
SSD_GROUPS = 8
SSD_STATE = 128
SSD_CONV = 5
SSD_CHUNK = 128
HEADS_PER_GROUP = SSD_HEADS // SSD_GROUPS
GROUP_WIDTH = HEADS_PER_GROUP * SSD_HEAD_DIM
DT_ROWS = 2 * HEADS_PER_GROUP

GMLP_WIDTH = D_MODEL
GMLP_GROUPS = 8
GMLP_CHUNK = 128
GMLP_GROUP_WIDTH = GMLP_WIDTH // GMLP_GROUPS

XATTN_HEADS = 4
XATTN_HEAD_DIM = D_MODEL // XATTN_HEADS
MEM_LEN = 256

LANES = 128
BF16_ROWS = 16

COL_BLOCK = 1024
BLK_Z, BLK_XS, BLK_BM, BLK_CM, BLK_GATE, BLK_U, BLK_V, BLK_XQ, BLK_XGATE, BLK_MERGE = 0, 2, 4, 5, 6, 7, 8, 9, 10, 11
N_COL_BLOCKS = 14
PROJ_COLS = N_COL_BLOCKS * COL_BLOCK
HALO = BF16_ROWS
CONV_PAD = SSD_CONV // 2


def _silu(x):
    return x * jax.nn.sigmoid(x)


def _gelu_tanh(x):
    c = 0.7978845608028654
    return x * (0.5 * (1.0 + jnp.tanh(c * (x + 0.044715 * (x * x * x)))))


def _rms(x, g):
    return x * lax.rsqrt(jnp.mean(x * x, axis=-1, keepdims=True) + EPS) * g


def _inproj_kernel(xprev_ref, x_ref, xnext_ref, gpre_ref, w_ref, wdth_ref, wdtl_ref,
                   convw_ref, convb_ref, lng_ref, lnb_ref,
                   out_ref, dt_ref, h_scr, h32_scr, hperm_scr, ynat_scr, *, tm, sub, csub, tiles_per_seq):
    i = pl.program_id(0)
    j = pl.program_id(1)
    n_sub = tm // sub
    n_lb = D_MODEL // LANES
    n_win = tm // csub
    ns = (csub + 2 * HALO) // 8
    nt_dims = (((1,), (1,)), ((), ()))

    def put_h32(row0, h):
        for lb in range(n_lb):
            h32_scr[lb, row0:row0 + h.shape[0], :] = h[:, lb * LANES:(lb + 1) * LANES]

    @pl.when(j == 0)
    def _():
        g = gpre_ref[...]
        first = (i % tiles_per_seq) == 0
        last = (i % tiles_per_seq) == tiles_per_seq - 1
        put_h32(0, _rms(xprev_ref[...], g) * jnp.where(first, 0.0, 1.0))
        put_h32(HALO + tm, _rms(xnext_ref[...], g) * jnp.where(last, 0.0, 1.0))
        for s in range(n_sub):
            h = _rms(x_ref[s * sub:(s + 1) * sub, :], g)
            hb = h.astype(BF16)
            h_scr[s * sub:(s + 1) * sub, :] = hb
            put_h32(HALO + s * sub, h)
            hl = (h - hb.astype(F32)).astype(BF16)
            dt = (lax.dot_general(wdth_ref[...], hb, nt_dims, preferred_element_type=F32)
                  + lax.dot_general(wdth_ref[...], hl, nt_dims, preferred_element_type=F32)
                  + lax.dot_general(wdtl_ref[...], hb, nt_dims, preferred_element_type=F32))
            for c in range(sub // SSD_CHUNK):
                dt_ref[s * (sub // SSD_CHUNK) + c] = dt[:, c * SSD_CHUNK:(c + 1) * SSD_CHUNK]
        for s in range(n_win):
            for b in range(0, ns, 2):
                slabs = [jnp.concatenate([h32_scr[lb, pl.ds(s * csub + b + d, 8, stride=ns), :]
                                          for lb in range(n_lb)], axis=1) for d in range(2)]
                hperm_scr[s, 8 * b:8 * b + BF16_ROWS, :] = jnp.concatenate(slabs, axis=0).astype(BF16)

    def plain(act):
        for s in range(n_sub):
            rows = h_scr[s * sub:(s + 1) * sub, :]
            acc = jnp.dot(rows, w_ref[...], preferred_element_type=F32)
            out_ref[s * sub:(s + 1) * sub, :] = act(acc).astype(BF16)

    is_silu = (j == BLK_Z) | (j == BLK_Z + 1) | (j == BLK_GATE) | (j == BLK_XGATE)
    is_conv = (j >= BLK_XS) & (j <= BLK_CM)
    is_sig = j >= BLK_MERGE

    @pl.when(is_silu)
    def _():
        plain(_silu)

    @pl.when(j == BLK_XQ)
    def _():
        plain(lambda a: a)

    @pl.when(j == BLK_U)
    def _():
        plain(_gelu_tanh)

    @pl.when(j == BLK_V)
    def _():
        def gelu_ln(a):
            v = _gelu_tanh(a)
            mu = jnp.mean(v, axis=-1, keepdims=True)
            vc = v - mu
            return vc * lax.rsqrt(jnp.mean(vc * vc, axis=-1, keepdims=True) + EPS) * lng_ref[...] + lnb_ref[...]
        plain(gelu_ln)

    @pl.when(is_sig)
    def _():
        plain(jax.nn.sigmoid)

    @pl.when(is_conv)
    def _():
        taps = [convw_ref[k:k + 1, :].reshape(1, 1, COL_BLOCK) for k in range(SSD_CONV)]
        project = lambda s: jnp.dot(hperm_scr[s], w_ref[...], preferred_element_type=F32).reshape(ns, 8, COL_BLOCK)
        nxt = project(0)
        for s in range(n_win):
            acc = nxt
            if s + 1 < n_win:
                nxt = project(s + 1)
            ext = jnp.concatenate([pltpu.roll(acc[ns - CONV_PAD:ns], 1, 1), acc,
                                   pltpu.roll(acc[0:CONV_PAD], 8 - 1, 1)], axis=0)
            y = convb_ref[...].reshape(1, 1, COL_BLOCK)
            for k in range(SSD_CONV):
                y = y + taps[k] * ext[k:k + ns]
            for b in range(ns):
                for lb in range(n_lb):
                    ynat_scr[s % 2, lb, pl.ds(b, 8, stride=ns), :] = y[b][:, lb * LANES:(lb + 1) * LANES]
            y = jnp.concatenate([ynat_scr[s % 2, lb, HALO:HALO + csub, :] for lb in range(n_lb)], axis=1)
            out_ref[s * csub:(s + 1) * csub, :] = _silu(y).astype(BF16)


def _inproj(x2, gpre, w_main, wdt_hi, wdt_lo, convw, convb, lng, lnb, *, seq_len, tm=2048, sub=256, csub=512):
    t = x2.shape[0]
    assert t % tm == 0 and seq_len % tm == 0 and tm % sub == 0 and sub % SSD_CHUNK == 0 and tm % csub == 0
    win = csub + 2 * HALO
    assert win % 8 == 0 and (win // 8) % 8 != 0
    n_tiles = t // tm
    halo_blocks = t // HALO
    per_tile = tm // HALO
    kernel = functools.partial(_inproj_kernel, tm=tm, sub=sub, csub=csub, tiles_per_seq=seq_len // tm)
    conv_idx = lambda i, j: (0, jnp.clip(j - BLK_XS, 0, BLK_CM - BLK_XS))
    return pl.pallas_call(
        kernel,
        grid=(n_tiles, N_COL_BLOCKS),
        in_specs=[
            pl.BlockSpec((HALO, D_MODEL), lambda i, j: (jnp.maximum(i * per_tile - 1, 0), 0)),
            pl.BlockSpec((tm, D_MODEL), lambda i, j: (i, 0)),
            pl.BlockSpec((HALO, D_MODEL), lambda i, j: (jnp.minimum((i + 1) * per_tile, halo_blocks - 1), 0)),
            pl.BlockSpec((1, D_MODEL), lambda i, j: (0, 0)),
            pl.BlockSpec((D_MODEL, COL_BLOCK), lambda i, j: (0, j)),
            pl.BlockSpec((2 * SSD_HEADS, D_MODEL), lambda i, j: (0, 0)),
            pl.BlockSpec((2 * SSD_HEADS, D_MODEL), lambda i, j: (0, 0)),
            pl.BlockSpec((SSD_CONV, COL_BLOCK), conv_idx),
            pl.BlockSpec((1, COL_BLOCK), conv_idx),
            pl.BlockSpec((1, COL_BLOCK), lambda i, j: (0, 0)),
            pl.BlockSpec((1, COL_BLOCK), lambda i, j: (0, 0)),
        ],
        out_specs=[
            pl.BlockSpec((tm, COL_BLOCK), lambda i, j: (i, j)),
            pl.BlockSpec((tm // SSD_CHUNK, 2 * SSD_HEADS, SSD_CHUNK), lambda i, j: (i, 0, 0)),
        ],
        out_shape=[
            jax.ShapeDtypeStruct((t, PROJ_COLS), BF16),
            jax.ShapeDtypeStruct((t // SSD_CHUNK, 2 * SSD_HEADS, SSD_CHUNK), F32),
        ],
        scratch_shapes=[
            pltpu.VMEM((tm, D_MODEL), BF16),
            pltpu.VMEM((D_MODEL // LANES, tm + 2 * HALO, LANES), F32),
            pltpu.VMEM((tm // csub, win, D_MODEL), BF16),
            pltpu.VMEM((2, COL_BLOCK // LANES, win, LANES), F32),
        ],
        compiler_params=pltpu.CompilerParams(
            dimension_semantics=("parallel", "arbitrary"),
            vmem_limit_bytes=58 * 1024 * 1024),
        name="inproj",
    )(x2, x2, x2, gpre, w_main, wdt_hi, wdt_lo, convw, convb, lng, lnb)


TRI_BLOCKS = 5
EXPAND_BLOCKS = 4
COL_CS, COL_ECS, COL_W = 0, DT_ROWS, 2 * DT_ROWS


def _softplus(x):
    return jnp.maximum(x, 0.0) + jnp.log1p(jnp.exp(-jnp.abs(x)))


def _dt_rows(dt_raw, dt_bias, a, tri):
    rows, q = dt_raw.shape
    dt = _softplus(dt_raw + dt_bias)
    da = dt * a
    hi = da.astype(BF16).astype(F32)
    r1 = da - hi
    mid = r1.astype(BF16).astype(F32)
    lo = r1 - mid
    parts = [hi, mid, lo]
    if (3 * rows) % BF16_ROWS:
        parts.append(jnp.zeros((BF16_ROWS - (3 * rows) % BF16_ROWS, q), F32))
    sums = jnp.dot(jnp.concatenate(parts, axis=0).astype(BF16), tri, preferred_element_type=F32)
    s = sums[0:rows] + sums[rows:2 * rows] + sums[2 * rows:3 * rows]
    fwd = lax.broadcasted_iota(jnp.int32, (rows, q), 0) % DT_ROWS < HEADS_PER_GROUP
    cs = jnp.where(fwd, s[:, 0:q], s[:, q:2 * q])
    rem = jnp.where(fwd, s[:, 2 * q:3 * q], s[:, 3 * q:4 * q])
    total = s[:, 4 * q:5 * q]
    return dt, cs, jnp.exp(cs), dt * jnp.exp(rem), jnp.exp(total)


def _token_columns(cs, ecs, w):
    pad = jnp.zeros((SSD_CHUNK - 3 * DT_ROWS, SSD_CHUNK), F32)
    return jnp.transpose(jnp.concatenate([cs, ecs, w, pad], axis=0))


def _head_row(v, d):
    lane = lax.broadcasted_iota(jnp.int32, (1, LANES), 1)
    r = d * HEADS_PER_GROUP
    lo = jnp.where(lane < SSD_HEAD_DIM, v[r:r + 1], v[r + 1:r + 2])
    hi = jnp.where(lane < SSD_HEAD_DIM, v[r + 2:r + 3], v[r + 3:r + 4])
    return jnp.concatenate([lo, hi], axis=1)


def _ssd_kernel(xs0_ref, bm0_ref, dt0_ref, dtb0_ref, alog0_ref,
                xs_ref, bm_ref, cm_ref, z_ref, dt_ref, dtb_ref, alog_ref, dskip_ref, ng_ref,
                tri_ref, exp_ref, y_ref, hb_all, hf_scr, hb_scr, *, cpb):
    k = pl.program_id(0)
    i = pl.program_id(1)
    nb = pl.num_programs(1)
    q = SSD_CHUNK
    gw = GROUP_WIDTH
    nr = cpb * DT_ROWS
    chunks = range(cpb)
    rows_of = lambda c: slice(c * q, (c + 1) * q)
    heads_of = lambda v, c: v[c * DT_ROWS:(c + 1) * DT_ROWS]
    wslot = k % 2
    rslot = 1 - wslot

    @pl.when((k == 0) & (i == 0))
    def _():
        hb_all[1] = jnp.zeros(hb_all.shape[1:], hb_all.dtype)

    @pl.when(i == 0)
    def _():
        hb_scr[...] = jnp.zeros_like(hb_scr)
        hf_scr[...] = jnp.zeros_like(hf_scr)

    nt_dims = (((1,), (1,)), ((), ()))
    scores = [lax.dot_general(cm_ref[rows_of(c), :], bm_ref[rows_of(c), :], nt_dims,
                              preferred_element_type=F32) for c in chunks]
    tile = lambda v: jnp.concatenate([v] * cpb, axis=0)
    dt_raw = jnp.concatenate([dt0_ref[...].reshape(nr, q), dt_ref[...].reshape(nr, q)], axis=0)
    bias = jnp.concatenate([tile(dtb0_ref[0]), tile(dtb_ref[0])], axis=0)
    a = jnp.concatenate([tile(-jnp.exp(alog0_ref[0])), tile(-jnp.exp(alog_ref[0]))], axis=0)
    dt, cs, ecs, w, cd = _dt_rows(dt_raw, bias, a, tri_ref[...])
    cols = [_token_columns(heads_of(cs, c), heads_of(ecs, c), heads_of(w, c)) for c in range(2 * cpb)]
    cols0, cols = cols[:cpb], cols[cpb:]
    cd0 = cd[:nr]
    dt, cs, cd = (v[nr:] for v in (dt, cs, cd))

    wexp0 = jnp.dot(jnp.concatenate(cols0, axis=0).astype(BF16), exp_ref[:, 3 * gw:4 * gw],
                    preferred_element_type=F32)
    expd = jnp.dot(jnp.concatenate(cols, axis=0).astype(BF16), exp_ref[:, 0:3 * gw],
                   preferred_element_type=F32)

    def local_state(bm, xs, wexp):
        bmt = jnp.transpose(bm.astype(F32)).astype(BF16)
        return jnp.dot(bmt, (xs.astype(F32) * wexp).astype(BF16), preferred_element_type=F32)

    st0 = [local_state(bm0_ref[rows_of(c), :], xs0_ref[rows_of(c), :], wexp0[rows_of(c)]) for c in chunks]
    h = hb_scr[...]
    for c in reversed(chunks):
        hb_all[wslot, (nb - 1 - i) * cpb + c] = h.astype(BF16)
        h = h * _head_row(heads_of(cd0, c), 1) + st0[c]
    hb_scr[...] = h

    t_idx = lax.broadcasted_iota(jnp.int32, (q, q), 0)
    s_idx = lax.broadcasted_iota(jnp.int32, (q, q), 1)
    lower = s_idx < t_idx
    diag = s_idx == t_idx
    lane_head = lax.broadcasted_iota(jnp.int32, (q, gw), 1) // SSD_HEAD_DIM

    def intra_chunk(c, sc):
        dt_c, cs_c = heads_of(dt, c), heads_of(cs, c)
        xs_b = xs_ref[rows_of(c), :]
        cols_c = cols[c]
        y = jnp.zeros((q, gw), F32)
        for r in range(HEADS_PER_GROUP):
            rb = HEADS_PER_GROUP + r
            colf = jnp.broadcast_to(cols_c[:, COL_CS + r:COL_CS + r + 1], (q, q))
            colb = jnp.broadcast_to(cols_c[:, COL_CS + rb:COL_CS + rb + 1], (q, q))
            arg = jnp.where(lower, colf - cs_c[r:r + 1], colb - cs_c[rb:rb + 1])
            dsel = jnp.where(lower, dt_c[r:r + 1],
                             jnp.where(diag, dt_c[r:r + 1] + dt_c[rb:rb + 1], dt_c[rb:rb + 1]))
            m = (sc * jnp.exp(arg) * dsel).astype(BF16)
            x_r = jnp.where(lane_head == r, xs_b, jnp.zeros_like(xs_b))
            y = y + jnp.dot(m, x_r, preferred_element_type=F32)
        return y

    def finish(c, y, yoff):
        ex = expd[rows_of(c)]
        y = y + yoff[:, 0:gw] * ex[:, 0:gw] + yoff[:, gw:2 * gw] * ex[:, gw:2 * gw]
        y = y + dskip_ref[...] * xs_ref[rows_of(c), :].astype(F32)
        y = y * z_ref[rows_of(c), :].astype(F32)
        y = y * lax.rsqrt(jnp.mean(y * y, axis=-1, keepdims=True) + EPS) * ng_ref[...]
        y_ref[rows_of(c), :] = y.astype(BF16)

    st = [local_state(bm_ref[rows_of(c), :], xs_ref[rows_of(c), :], expd[rows_of(c), 2 * gw:3 * gw])
          for c in chunks]
    h = hf_scr[...]
    yoff = []
    for c in chunks:
        hcat = jnp.concatenate([h.astype(BF16), hb_all[rslot, i * cpb + c]], axis=1)
        yoff.append(jnp.dot(cm_ref[rows_of(c), :], hcat, preferred_element_type=F32))
        h = h * _head_row(heads_of(cd, c), 0) + st[c]
    hf_scr[...] = h
    for c in chunks:
        finish(c, intra_chunk(c, scores[c]), yoff[c])


def _ssd_constants():
    q = SSD_CHUNK
    u = jnp.arange(q)[:, None]
    s = jnp.arange(q)[None, :]
    tri = jnp.concatenate([u <= s, u >= s, u > s, u < s, jnp.ones((q, q), bool)], axis=1).astype(BF16)
    row = jnp.arange(q)[:, None]
    col = jnp.arange(EXPAND_BLOCKS * GROUP_WIDTH)[None, :]
    src = jnp.array([COL_ECS, COL_ECS + HEADS_PER_GROUP, COL_W, COL_W + HEADS_PER_GROUP])[col // GROUP_WIDTH]
    expand = (row == src + (col % GROUP_WIDTH) // SSD_HEAD_DIM).astype(BF16)
    return tri, expand


def _ssd(proj, dt4, dtb, alog, dskip, ng, *, batch, seq_len, tq=1024):
    t = proj.shape[0]
    assert seq_len % tq == 0 and tq % SSD_CHUNK == 0
    nb = seq_len // tq
    cpb = tq // SSD_CHUNK
    tri, expand = _ssd_constants()
    gw = GROUP_WIDTH
    xs0, bm0, cm0 = BLK_XS * COL_BLOCK // gw, BLK_BM * COL_BLOCK // SSD_STATE, BLK_CM * COL_BLOCK // SSD_STATE

    n_pairs = batch * SSD_GROUPS

    def back(k):
        p = jnp.minimum(k, n_pairs - 1)
        return p // SSD_GROUPS, p % SSD_GROUPS

    def out(k):
        p = jnp.maximum(k - 1, 0)
        return p // SSD_GROUPS, p % SSD_GROUPS

    def back_rows(k, i):
        return back(k)[0] * nb + nb - 1 - i

    def out_rows(k, i):
        return out(k)[0] * nb + i

    def dt_specs(pair):
        rows = back_rows if pair is back else out_rows
        grp = (1, DT_ROWS, LANES)
        return [
            pl.BlockSpec((cpb, DT_ROWS, SSD_CHUNK), lambda k, i: (rows(k, i), pair(k)[1], 0)),
            pl.BlockSpec(grp, lambda k, i: (pair(k)[1], 0, 0)),
            pl.BlockSpec(grp, lambda k, i: (pair(k)[1], 0, 0)),
        ]

    const2 = lambda k, i: (0, 0)
    return pl.pallas_call(
        functools.partial(_ssd_kernel, cpb=cpb),
        grid=(n_pairs + 1, nb),
        in_specs=[
            pl.BlockSpec((tq, gw), lambda k, i: (back_rows(k, i), xs0 + back(k)[1])),
            pl.BlockSpec((tq, SSD_STATE), lambda k, i: (back_rows(k, i), bm0 + back(k)[1])),
            *dt_specs(back),
            pl.BlockSpec((tq, gw), lambda k, i: (out_rows(k, i), xs0 + out(k)[1])),
            pl.BlockSpec((tq, SSD_STATE), lambda k, i: (out_rows(k, i), bm0 + out(k)[1])),
            pl.BlockSpec((tq, SSD_STATE), lambda k, i: (out_rows(k, i), cm0 + out(k)[1])),
            pl.BlockSpec((tq, gw), lambda k, i: (out_rows(k, i), out(k)[1])),
            *dt_specs(out),
            pl.BlockSpec((1, gw), lambda k, i: (0, out(k)[1])),
            pl.BlockSpec((1, gw), lambda k, i: (0, out(k)[1])),
            pl.BlockSpec((SSD_CHUNK, TRI_BLOCKS * SSD_CHUNK), const2),
            pl.BlockSpec((SSD_CHUNK, EXPAND_BLOCKS * gw), const2),
        ],
        out_specs=pl.BlockSpec((tq, gw), lambda k, i: (jnp.where(k == 0, t // tq, out_rows(k, i)),
                                                       jnp.where(k == 0, i % SSD_GROUPS, out(k)[1]))),
        out_shape=jax.ShapeDtypeStruct((t + tq, SSD_WIDTH), BF16),
        scratch_shapes=[
            pltpu.VMEM((2, seq_len // SSD_CHUNK, SSD_STATE, gw), BF16),
            pltpu.VMEM((SSD_STATE, gw), F32),
            pltpu.VMEM((SSD_STATE, gw), F32),
        ],
        compiler_params=pltpu.CompilerParams(
            dimension_semantics=("arbitrary", "arbitrary"),
            vmem_limit_bytes=48 * 1024 * 1024),
        name="ssd",
    )(proj, proj, dt4, dtb, alog, proj, proj, proj, proj, dt4, dtb, alog, dskip, ng, tri, expand)


def _memkv_kernel(mem_ref, g_ref, w_ref, kv_ref):
    m = _rms(mem_ref[0], g_ref[...]).astype(BF16)
    kv_ref[0] = jnp.dot(m, w_ref[...], preferred_element_type=F32).astype(BF16)


def _memkv(mem, g, w_kv):
    b = mem.shape[0]
    return pl.pallas_call(
        _memkv_kernel,
        grid=(b,),
        in_specs=[
            pl.BlockSpec((1, MEM_LEN, D_MODEL), lambda i: (i, 0, 0)),
            pl.BlockSpec((1, D_MODEL), lambda i: (0, 0)),
            pl.BlockSpec((D_MODEL, 2 * D_MODEL), lambda i: (0, 0)),
        ],
        out_specs=pl.BlockSpec((1, MEM_LEN, 2 * D_MODEL), lambda i: (i, 0, 0)),
        out_shape=jax.ShapeDtypeStruct((b, MEM_LEN, 2 * D_MODEL), BF16),
        compiler_params=pltpu.CompilerParams(dimension_semantics=("parallel",)),
        name="memkv",
    )(mem, g, w_kv)


def _tail_kernel(yssd_ref, gate_ref, u_ref, v_ref, xq_ref, xgate_ref, m0_ref, m1_ref, m2_ref, x_ref,
                 k_ref, vmem_ref, ws_ref, bexp_ref, wbs_ref, wbg_ref, wbx_ref, wout_ref, gpost_ref,
                 out_ref, sv_scr, o_scr, *, tq):
    n_chunks = tq // GMLP_CHUNK
    gc = GMLP_CHUNK
    gwid = GMLP_GROUP_WIDTH

    for g in range(GMLP_GROUPS):
        cols = slice(g * gwid, (g + 1) * gwid)
        vcat = jnp.concatenate([v_ref[c * gc:(c + 1) * gc, cols] for c in range(n_chunks)], axis=1)
        sv = jnp.dot(ws_ref[g], vcat, preferred_element_type=F32)
        for c in range(n_chunks):
            sv_scr[c * gc:(c + 1) * gc, cols] = sv[:, c * gc:(c + 1) * gc]
    for c in range(n_chunks):
        rows = slice(c * gc, (c + 1) * gc)
        sv_scr[rows, :] = (u_ref[rows, :].astype(F32) * (sv_scr[rows, :] + bexp_ref[...])
                           * gate_ref[rows, :].astype(F32))

    nt_dims = (((1,), (1,)), ((), ()))
    scale = XATTN_HEAD_DIM ** -0.5
    for h in range(XATTN_HEADS):
        hc = slice(h * XATTN_HEAD_DIM, (h + 1) * XATTN_HEAD_DIM)
        s = lax.dot_general(xq_ref[:, hc], k_ref[0, :, hc], nt_dims, preferred_element_type=F32) * scale
        e = jnp.exp(s - jnp.max(s, axis=-1, keepdims=True))
        p = e / jnp.sum(e, axis=-1, keepdims=True)
        o = jnp.dot(p.astype(BF16), vmem_ref[0, :, hc], preferred_element_type=F32)
        o_scr[:, hc] = o * xgate_ref[:, hc].astype(F32)

    merged = m0_ref[...].astype(F32) * jnp.dot(yssd_ref[...], wbs_ref[...], preferred_element_type=F32)
    merged = merged + m1_ref[...].astype(F32) * jnp.dot(sv_scr[...].astype(BF16), wbg_ref[...],
                                                        preferred_element_type=F32)
    merged = merged + m2_ref[...].astype(F32) * jnp.dot(o_scr[...].astype(BF16), wbx_ref[...],
                                                        preferred_element_type=F32)
    out = jnp.dot(merged.astype(BF16), wout_ref[...], preferred_element_type=F32)
    out_ref[...] = x_ref[...] + _rms(out, gpost_ref[...])


def _tail(y_ssd, proj, x2, kv, ws, bexp, wbs, wbg, wbx, wout, gpost, *, seq_len, tq=512):
    t = x2.shape[0]
    assert seq_len % tq == 0 and tq % GMLP_CHUNK == 0
    per_seq = seq_len // tq
    resident = functools.partial(pl.BlockSpec, pipeline_mode=pl.Buffered(1))

    def proj_block(blk):
        return pl.BlockSpec((tq, COL_BLOCK), lambda i: (i, blk))

    return pl.pallas_call(
        functools.partial(_tail_kernel, tq=tq),
        grid=(t // tq,),
        in_specs=[
            pl.BlockSpec((tq, SSD_WIDTH), lambda i: (i, 0)),
            proj_block(BLK_GATE), proj_block(BLK_U), proj_block(BLK_V), proj_block(BLK_XQ), proj_block(BLK_XGATE),
            proj_block(BLK_MERGE), proj_block(BLK_MERGE + 1), proj_block(BLK_MERGE + 2),
            pl.BlockSpec((tq, D_MODEL), lambda i: (i, 0)),
            pl.BlockSpec((1, MEM_LEN, D_MODEL), lambda i: (i // per_seq, 0, 0)),
            pl.BlockSpec((1, MEM_LEN, D_MODEL), lambda i: (i // per_seq, 0, 1)),
            resident((GMLP_GROUPS, GMLP_CHUNK, GMLP_CHUNK), lambda i: (0, 0, 0)),
            resident((GMLP_CHUNK, GMLP_WIDTH), lambda i: (0, 0)),
            resident((SSD_WIDTH, D_MODEL), lambda i: (0, 0)),
            resident((GMLP_WIDTH, D_MODEL), lambda i: (0, 0)),
            resident((D_MODEL, D_MODEL), lambda i: (0, 0)),
            resident((D_MODEL, D_MODEL), lambda i: (0, 0)),
            resident((1, D_MODEL), lambda i: (0, 0)),
        ],
        out_specs=pl.BlockSpec((tq, D_MODEL), lambda i: (i, 0)),
        out_shape=jax.ShapeDtypeStruct((t, D_MODEL), F32),
        scratch_shapes=[
            pltpu.VMEM((tq, GMLP_WIDTH), F32),
            pltpu.VMEM((tq, D_MODEL), F32),
        ],
        compiler_params=pltpu.CompilerParams(
            dimension_semantics=("parallel",),
            vmem_limit_bytes=56 * 1024 * 1024),
        name="tail",
    )(y_ssd, proj, proj, proj, proj, proj, proj, proj, proj, x2, kv, kv,
      ws, bexp, wbs, wbg, wbx, wout, gpost)


def _split_bf16(w):
    hi = w.astype(BF16)
    return hi, (w - hi.astype(F32)).astype(BF16)


def _prep_layer(p, l):
    w_in = p['w_in'][l]
    dt0 = SSD_WIDTH + SSD_WIDTH + 2 * SSD_GROUPS * SSD_STATE
    dt1 = dt0 + 2 * SSD_HEADS
    w_main = jnp.concatenate([w_in[:, :dt0], w_in[:, dt1:]], axis=1).astype(BF16)
    w_dt = w_in[:, dt0:dt1].reshape(D_MODEL, 2, SSD_GROUPS, HEADS_PER_GROUP)
    w_dt = w_dt.transpose(2, 1, 3, 0).reshape(2 * SSD_HEADS, D_MODEL)
    wdt_hi, wdt_lo = _split_bf16(w_dt)

    def per_group_rows(v):
        v = v.reshape(2, SSD_GROUPS, HEADS_PER_GROUP).transpose(1, 0, 2).reshape(SSD_GROUPS, DT_ROWS, 1)
        return jnp.broadcast_to(v, (SSD_GROUPS, DT_ROWS, LANES))

    return dict(
        dtb=per_group_rows(p['dt_bias'][l]), alog=per_group_rows(p['a_log'][l]),
        dskip=jnp.repeat(p['d_skip'][l], SSD_HEAD_DIM).reshape(1, SSD_WIDTH),
        ssd_ng=p['ssd_norm_g'][l].reshape(1, SSD_WIDTH),
        gpre=p['norm_pre_g'][l].reshape(1, D_MODEL),
        w_main=w_main, wdt_hi=wdt_hi, wdt_lo=wdt_lo,
        convw=p['conv_w'][l], convb=p['conv_b'][l].reshape(1, -1),
        lng=p['gmlp_ln_g'][l].reshape(1, -1), lnb=p['gmlp_ln_b'][l].reshape(1, -1),
        ws=p['w_spatial'][l].astype(BF16),
        bexp=jnp.repeat(p['b_spatial'][l].T, GMLP_GROUP_WIDTH, axis=1),
        mem_g=p['mem_norm_g'][l].reshape(1, D_MODEL), w_kv=p['w_kv'][l].astype(BF16),
        wbs=p['w_br_ssd'][l].astype(BF16), wbg=p['w_br_gmlp'][l].astype(BF16),
        wbx=p['w_br_xattn'][l].astype(BF16), wout=p['w_out'][l].astype(BF16),
        gpost=p['norm_post_g'][l].reshape(1, D_MODEL),
    )


def _layer(x2, mem, prm, *, batch, seq_len):
    proj, dt4 = _inproj(x2, prm['gpre'], prm['w_main'], prm['wdt_hi'], prm['wdt_lo'],
                        prm['convw'], prm['convb'], prm['lng'], prm['lnb'], seq_len=seq_len)
    y_ssd = _ssd(proj, dt4, prm['dtb'], prm['alog'], prm['dskip'], prm['ssd_ng'], batch=batch, seq_len=seq_len)
    kv = _memkv(mem, prm['mem_g'], prm['w_kv'])
    return _tail(y_ssd, proj, x2, kv, prm['ws'], prm['bexp'], prm['wbs'], prm['wbg'], prm['wbx'], prm['wout'],
                 prm['gpost'], seq_len=seq_len)


def kernel(x, mem, norm_pre_g, w_in, conv_w, conv_b, dt_bias, a_log, d_skip, ssd_norm_g, gmlp_ln_g, gmlp_ln_b, w_spatial, b_spatial, mem_norm_g, w_kv, w_br_ssd, w_br_gmlp, w_br_xattn, w_out, norm_post_g):
    p = dict(norm_pre_g=norm_pre_g, w_in=w_in, conv_w=conv_w, conv_b=conv_b, dt_bias=dt_bias, a_log=a_log,
             d_skip=d_skip, ssd_norm_g=ssd_norm_g, gmlp_ln_g=gmlp_ln_g, gmlp_ln_b=gmlp_ln_b, w_spatial=w_spatial,
             b_spatial=b_spatial, mem_norm_g=mem_norm_g, w_kv=w_kv, w_br_ssd=w_br_ssd, w_br_gmlp=w_br_gmlp,
             w_br_xattn=w_br_xattn, w_out=w_out, norm_post_g=norm_post_g)
    batch, seq_len, _ = x.shape
    x2 = x.reshape(batch * seq_len, D_MODEL)
    for l in range(w_in.shape[0]):
        x2 = _layer(x2, mem, _prep_layer(p, l), batch=batch, seq_len=seq_len)
    return x2.reshape(batch, seq_len, D_MODEL)
```

```python
import functools

import jax
import jax.numpy as jnp
from jax import lax
from jax.experimental import pallas as pl
from jax.experimental.pallas import tpu as pltpu

F32 = jnp.float32
BF16 = jnp.bfloat16

EPS = 1e-6
D_MODEL = 1024
N_BRANCH = 3

SSD_WIDTH = 2 * D_MODEL
SSD_HEAD_DIM = 64
SSD_HEADS = SSD_WIDTH // SSD_HEAD_DIM
SSD_GROUPS = 8
SSD_STATE = 128
SSD_CONV = 5
SSD_CHUNK = 128
HEADS_PER_GROUP = SSD_HEADS // SSD_GROUPS
GROUP_WIDTH = HEADS_PER_GROUP * SSD_HEAD_DIM
DT_ROWS = 2 * HEADS_PER_GROUP

GMLP_WIDTH = D_MODEL
GMLP_GROUPS = 8
GMLP_CHUNK = 128
GMLP_GROUP_WIDTH = GMLP_WIDTH // GMLP_GROUPS

XATTN_HEADS = 4
XATTN_HEAD_DIM = D_MODEL // XATTN_HEADS
MEM_LEN = 256

LANES = 128
BF16_ROWS = 16

COL_BLOCK = 1024
BLK_Z, BLK_XS, BLK_BM, BLK_CM, BLK_GATE, BLK_U, BLK_V, BLK_XQ, BLK_XGATE, BLK_MERGE = 0, 2, 4, 5, 6, 7, 8, 9, 10, 11
N_COL_BLOCKS = 14
PROJ_COLS = N_COL_BLOCKS * COL_BLOCK
HALO = BF16_ROWS
CONV_PAD = SSD_CONV // 2


def _silu(x):
    return x * jax.nn.sigmoid(x)


def _gelu_tanh(x):
    c = 0.7978845608028654
    return x * (0.5 * (1.0 + jnp.tanh(c * (x + 0.044715 * (x * x * x)))))


def _rms(x, g):
    return x * lax.rsqrt(jnp.mean(x * x, axis=-1, keepdims=True) + EPS) * g


def _inproj_kernel(xprev_ref, x_ref, xnext_ref, gpre_ref, w_ref, wdth_ref, wdtl_ref,
                   convw_ref, convb_ref, lng_ref, lnb_ref,
                   out_ref, dt_ref, h_scr, h32_scr, hperm_scr, ynat_scr, *, tm, sub, csub, tiles_per_seq):
    i = pl.program_id(0)
    j = pl.program_id(1)
    n_sub = tm // sub
    n_lb = D_MODEL // LANES
    n_win = tm // csub
    ns = (csub + 2 * HALO) // 8
    nt_dims = (((1,), (1,)), ((), ()))

    def put_h32(row0, h):
        for lb in range(n_lb):
            h32_scr[lb, row0:row0 + h.shape[0], :] = h[:, lb * LANES:(lb + 1) * LANES]

    @pl.when(j == 0)
    def _():
        g = gpre_ref[...]
        first = (i % tiles_per_seq) == 0
        last = (i % tiles_per_seq) == tiles_per_seq - 1
        put_h32(0, _rms(xprev_ref[...], g) * jnp.where(first, 0.0, 1.0))
        put_h32(HALO + tm, _rms(xnext_ref[...], g) * jnp.where(last, 0.0, 1.0))
        for s in range(n_sub):
            h = _rms(x_ref[s * sub:(s + 1) * sub, :], g)
            hb = h.astype(BF16)
            h_scr[s * sub:(s + 1) * sub, :] = hb
            put_h32(HALO + s * sub, h)
            hl = (h - hb.astype(F32)).astype(BF16)
            dt = (lax.dot_general(wdth_ref[...], hb, nt_dims, preferred_element_type=F32)
                  + lax.dot_general(wdth_ref[...], hl, nt_dims, preferred_element_type=F32)
                  + lax.dot_general(wdtl_ref[...], hb, nt_dims, preferred_element_type=F32))
            for c in range(sub // SSD_CHUNK):
                dt_ref[s * (sub // SSD_CHUNK) + c] = dt[:, c * SSD_CHUNK:(c + 1) * SSD_CHUNK]
        for s in range(n_win):
            for b in range(0, ns, 2):
                slabs = [jnp.concatenate([h32_scr[lb, pl.ds(s * csub + b + d, 8, stride=ns), :]
                                          for lb in range(n_lb)], axis=1) for d in range(2)]
                hperm_scr[s, 8 * b:8 * b + BF16_ROWS, :] = jnp.concatenate(slabs, axis=0).astype(BF16)

    def plain(act):
        for s in range(n_sub):
            rows = h_scr[s * sub:(s + 1) * sub, :]
            acc = jnp.dot(rows, w_ref[...], preferred_element_type=F32)
            out_ref[s * sub:(s + 1) * sub, :] = act(acc).astype(BF16)

    is_silu = (j == BLK_Z) | (j == BLK_Z + 1) | (j == BLK_GATE) | (j == BLK_XGATE)
    is_conv = (j >= BLK_XS) & (j <= BLK_CM)
    is_sig = j >= BLK_MERGE

    @pl.when(is_silu)
    def _():
        plain(_silu)

    @pl.when(j == BLK_XQ)
    def _():
        plain(lambda a: a)

    @pl.when(j == BLK_U)
    def _():
        plain(_gelu_tanh)

    @pl.when(j == BLK_V)
    def _():
        def gelu_ln(a):
            v = _gelu_tanh(a)
            mu = jnp.mean(v, axis=-1, keepdims=True)
            vc = v - mu
            return vc * lax.rsqrt(jnp.mean(vc * vc, axis=-1, keepdims=True) + EPS) * lng_ref[...] + lnb_ref[...]
        plain(gelu_ln)

    @pl.when(is_sig)
    def _():
        plain(jax.nn.sigmoid)

    @pl.when(is_conv)
    def _():
        taps = [convw_ref[k:k + 1, :].reshape(1, 1, COL_BLOCK) for k in range(SSD_CONV)]
        project = lambda s: jnp.dot(hperm_scr[s], w_ref[...], preferred_element_type=F32).reshape(ns, 8, COL_BLOCK)
        nxt = project(0)
        for s in range(n_win):
            acc = nxt
            if s + 1 < n_win:
                nxt = project(s + 1)
            ext = jnp.concatenate([pltpu.roll(acc[ns - CONV_PAD:ns], 1, 1), acc,
                                   pltpu.roll(acc[0:CONV_PAD], 8 - 1, 1)], axis=0)
            y = convb_ref[...].reshape(1, 1, COL_BLOCK)
            for k in range(SSD_CONV):
                y = y + taps[k] * ext[k:k + ns]
            for b in range(ns):
                for lb in range(n_lb):
                    ynat_scr[s % 2, lb, pl.ds(b, 8, stride=ns), :] = y[b][:, lb * LANES:(lb + 1) * LANES]
            y = jnp.concatenate([ynat_scr[s % 2, lb, HALO:HALO + csub, :] for lb in range(n_lb)], axis=1)
            out_ref[s * csub:(s + 1) * csub, :] = _silu(y).astype(BF16)


def _inproj(x2, gpre, w_main, wdt_hi, wdt_lo, convw, convb, lng, lnb, *, seq_len, tm=2048, sub=256, csub=512):
    t = x2.shape[0]
    assert t % tm == 0 and seq_len % tm == 0 and tm % sub == 0 and sub % SSD_CHUNK == 0 and tm % csub == 0
    win = csub + 2 * HALO
    assert win % 8 == 0 and (win // 8) % 8 != 0
    n_tiles = t // tm
    halo_blocks = t // HALO
    per_tile = tm // HALO
    kernel = functools.partial(_inproj_kernel, tm=tm, sub=sub, csub=csub, tiles_per_seq=seq_len // tm)
    conv_idx = lambda i, j: (0, jnp.clip(j - BLK_XS, 0, BLK_CM - BLK_XS))
    return pl.pallas_call(
        kernel,
        grid=(n_tiles, N_COL_BLOCKS),
        in_specs=[
            pl.BlockSpec((HALO, D_MODEL), lambda i, j: (jnp.maximum(i * per_tile - 1, 0), 0)),
            pl.BlockSpec((tm, D_MODEL), lambda i, j: (i, 0)),
            pl.BlockSpec((HALO, D_MODEL), lambda i, j: (jnp.minimum((i + 1) * per_tile, halo_blocks - 1), 0)),
            pl.BlockSpec((1, D_MODEL), lambda i, j: (0, 0)),
            pl.BlockSpec((D_MODEL, COL_BLOCK), lambda i, j: (0, j)),
            pl.BlockSpec((2 * SSD_HEADS, D_MODEL), lambda i, j: (0, 0)),
            pl.BlockSpec((2 * SSD_HEADS, D_MODEL), lambda i, j: (0, 0)),
            pl.BlockSpec((SSD_CONV, COL_BLOCK), conv_idx),
            pl.BlockSpec((1, COL_BLOCK), conv_idx),
            pl.BlockSpec((1, COL_BLOCK), lambda i, j: (0, 0)),
            pl.BlockSpec((1, COL_BLOCK), lambda i, j: (0, 0)),
        ],
        out_specs=[
            pl.BlockSpec((tm, COL_BLOCK), lambda i, j: (i, j)),
            pl.BlockSpec((tm // SSD_CHUNK, 2 * SSD_HEADS, SSD_CHUNK), lambda i, j: (i, 0, 0)),
        ],
        out_shape=[
            jax.ShapeDtypeStruct((t, PROJ_COLS), BF16),
            jax.ShapeDtypeStruct((t // SSD_CHUNK, 2 * SSD_HEADS, SSD_CHUNK), F32),
        ],
        scratch_shapes=[
            pltpu.VMEM((tm, D_MODEL), BF16),
            pltpu.VMEM((D_MODEL // LANES, tm + 2 * HALO, LANES), F32),
            pltpu.VMEM((tm // csub, win, D_MODEL), BF16),
            pltpu.VMEM((2, COL_BLOCK // LANES, win, LANES), F32),
        ],
        compiler_params=pltpu.CompilerParams(
            dimension_semantics=("parallel", "arbitrary"),
            vmem_limit_bytes=58 * 1024 * 1024),
        name="inproj",
    )(x2, x2, x2, gpre, w_main, wdt_hi, wdt_lo, convw, convb, lng, lnb)


TRI_BLOCKS = 5
EXPAND_BLOCKS = 4
COL_CS, COL_ECS, COL_W = 0, DT_ROWS, 2 * DT_ROWS


def _softplus(x):
    return jnp.maximum(x, 0.0) + jnp.log1p(jnp.exp(-jnp.abs(x)))


def _dt_rows(dt_raw, dt_bias, a, tri):
    rows, q = dt_raw.shape
    dt = _softplus(dt_raw + dt_bias)
    da = dt * a
    hi = da.astype(BF16).astype(F32)
    r1 = da - hi
    mid = r1.astype(BF16).astype(F32)
    lo = r1 - mid
    parts = [hi, mid, lo]
    if (3 * rows) % BF16_ROWS:
        parts.append(jnp.zeros((BF16_ROWS - (3 * rows) % BF16_ROWS, q), F32))
    sums = jnp.dot(jnp.concatenate(parts, axis=0).astype(BF16), tri, preferred_element_type=F32)
    s = sums[0:rows] + sums[rows:2 * rows] + sums[2 * rows:3 * rows]
    fwd = lax.broadcasted_iota(jnp.int32, (rows, q), 0) % DT_ROWS < HEADS_PER_GROUP
    cs = jnp.where(fwd, s[:, 0:q], s[:, q:2 * q])
    rem = jnp.where(fwd, s[:, 2 * q:3 * q], s[:, 3 * q:4 * q])
    total = s[:, 4 * q:5 * q]
    return dt, cs, jnp.exp(cs), dt * jnp.exp(rem), jnp.exp(total)


def _token_columns(cs, ecs, w):
    pad = jnp.zeros((SSD_CHUNK - 3 * DT_ROWS, SSD_CHUNK), F32)
    return jnp.transpose(jnp.concatenate([cs, ecs, w, pad], axis=0))


def _head_row(v, d):
    lane = lax.broadcasted_iota(jnp.int32, (1, LANES), 1)
    r = d * HEADS_PER_GROUP
    lo = jnp.where(lane < SSD_HEAD_DIM, v[r:r + 1], v[r + 1:r + 2])
    hi = jnp.where(lane < SSD_HEAD_DIM, v[r + 2:r + 3], v[r + 3:r + 4])
    return jnp.concatenate([lo, hi], axis=1)


def _ssd_kernel(xs0_ref, bm0_ref, dt0_ref, dtb0_ref, alog0_ref,
                xs_ref, bm_ref, cm_ref, z_ref, dt_ref, dtb_ref, alog_ref, dskip_ref, ng_ref,
                tri_ref, exp_ref, y_ref, hb_all, hf_scr, hb_scr, *, cpb):
    k = pl.program_id(0)
    i = pl.program_id(1)
    nb = pl.num_programs(1)
    q = SSD_CHUNK
    gw = GROUP_WIDTH
    nr = cpb * DT_ROWS
    chunks = range(cpb)
    rows_of = lambda c: slice(c * q, (c + 1) * q)
    heads_of = lambda v, c: v[c * DT_ROWS:(c + 1) * DT_ROWS]
    wslot = k % 2
    rslot = 1 - wslot

    @pl.when((k == 0) & (i == 0))
    def _():
        hb_all[1] = jnp.zeros(hb_all.shape[1:], hb_all.dtype)

    @pl.when(i == 0)
    def _():
        hb_scr[...] = jnp.zeros_like(hb_scr)
        hf_scr[...] = jnp.zeros_like(hf_scr)

    nt_dims = (((1,), (1,)), ((), ()))
    scores = [lax.dot_general(cm_ref[rows_of(c), :], bm_ref[rows_of(c), :], nt_dims,
                              preferred_element_type=F32) for c in chunks]
    tile = lambda v: jnp.concatenate([v] * cpb, axis=0)
    dt_raw = jnp.concatenate([dt0_ref[...].reshape(nr, q), dt_ref[...].reshape(nr, q)], axis=0)
    bias = jnp.concatenate([tile(dtb0_ref[0]), tile(dtb_ref[0])], axis=0)
    a = jnp.concatenate([tile(-jnp.exp(alog0_ref[0])), tile(-jnp.exp(alog_ref[0]))], axis=0)
    dt, cs, ecs, w, cd = _dt_rows(dt_raw, bias, a, tri_ref[...])
    cols = [_token_columns(heads_of(cs, c), heads_of(ecs, c), heads_of(w, c)) for c in range(2 * cpb)]
    cols0, cols = cols[:cpb], cols[cpb:]
    cd0 = cd[:nr]
    dt, cs, cd = (v[nr:] for v in (dt, cs, cd))

    wexp0 = jnp.dot(jnp.concatenate(cols0, axis=0).astype(BF16), exp_ref[:, 3 * gw:4 * gw],
                    preferred_element_type=F32)
    expd = jnp.dot(jnp.concatenate(cols, axis=0).astype(BF16), exp_ref[:, 0:3 * gw],
                   preferred_element_type=F32)

    def local_state(bm, xs, wexp):
        bmt = jnp.transpose(bm.astype(F32)).astype(BF16)
        return jnp.dot(bmt, (xs.astype(F32) * wexp).astype(BF16), preferred_element_type=F32)

    st0 = [local_state(bm0_ref[rows_of(c), :], xs0_ref[rows_of(c), :], wexp0[rows_of(c)]) for c in chunks]
    h = hb_scr[...]
    for c in reversed(chunks):
        hb_all[wslot, (nb - 1 - i) * cpb + c] = h.astype(BF16)
        h = h * _head_row(heads_of(cd0, c), 1) + st0[c]
    hb_scr[...] = h

    t_idx = lax.broadcasted_iota(jnp.int32, (q, q), 0)
    s_idx = lax.broadcasted_iota(jnp.int32, (q, q), 1)
    lower = s_idx < t_idx
    diag = s_idx == t_idx
    lane_head = lax.broadcasted_iota(jnp.int32, (q, gw), 1) // SSD_HEAD_DIM

    def intra_chunk(c, sc):
        dt_c, cs_c = heads_of(dt, c), heads_of(cs, c)
        xs_b = xs_ref[rows_of(c), :]
        cols_c = cols[c]
        y = jnp.zeros((q, gw), F32)
        for r in range(HEADS_PER_GROUP):
            rb = HEADS_PER_GROUP + r
            colf = jnp.broadcast_to(cols_c[:, COL_CS + r:COL_CS + r + 1], (q, q))
            colb = jnp.broadcast_to(cols_c[:, COL_CS + rb:COL_CS + rb + 1], (q, q))
            arg = jnp.where(lower, colf - cs_c[r:r + 1], colb - cs_c[rb:rb + 1])
            dsel = jnp.where(lower, dt_c[r:r + 1],
                             jnp.where(diag, dt_c[r:r + 1] + dt_c[rb:rb + 1], dt_c[rb:rb + 1]))
            m = (sc * jnp.exp(arg) * dsel).astype(BF16)
            x_r = jnp.where(lane_head == r, xs_b, jnp.zeros_like(xs_b))
            y = y + jnp.dot(m, x_r, preferred_element_type=F32)
        return y

    def finish(c, y, yoff):
        ex = expd[rows_of(c)]
        y = y + yoff[:, 0:gw] * ex[:, 0:gw] + yoff[:, gw:2 * gw] * ex[:, gw:2 * gw]
        y = y + dskip_ref[...] * xs_ref[rows_of(c), :].astype(F32)
        y = y * z_ref[rows_of(c), :].astype(F32)
        y = y * lax.rsqrt(jnp.mean(y * y, axis=-1, keepdims=True) + EPS) * ng_ref[...]
        y_ref[rows_of(c), :] = y.astype(BF16)

    st = [local_state(bm_ref[rows_of(c), :], xs_ref[rows_of(c), :], expd[rows_of(c), 2 * gw:3 * gw])
          for c in chunks]
    h = hf_scr[...]
    yoff = []
    for c in chunks:
        hcat = jnp.concatenate([h.astype(BF16), hb_all[rslot, i * cpb + c]], axis=1)
        yoff.append(jnp.dot(cm_ref[rows_of(c), :], hcat, preferred_element_type=F32))
        h = h * _head_row(heads_of(cd, c), 0) + st[c]
    hf_scr[...] = h
    for c in chunks:
        finish(c, intra_chunk(c, scores[c]), yoff[c])


def _ssd_constants():
    q = SSD_CHUNK
    u = jnp.arange(q)[:, None]
    s = jnp.arange(q)[None, :]
    tri = jnp.concatenate([u <= s, u >= s, u > s, u < s, jnp.ones((q, q), bool)], axis=1).astype(BF16)
    row = jnp.arange(q)[:, None]
    col = jnp.arange(EXPAND_BLOCKS * GROUP_WIDTH)[None, :]
    src = jnp.array([COL_ECS, COL_ECS + HEADS_PER_GROUP, COL_W, COL_W + HEADS_PER_GROUP])[col // GROUP_WIDTH]
    expand = (row == src + (col % GROUP_WIDTH) // SSD_HEAD_DIM).astype(BF16)
    return tri, expand


def _ssd(proj, dt4, dtb, alog, dskip, ng, *, batch, seq_len, tq=1024):
    t = proj.shape[0]
    assert seq_len % tq == 0 and tq % SSD_CHUNK == 0
    nb = seq_len // tq
    cpb = tq // SSD_CHUNK
    tri, expand = _ssd_constants()
    gw = GROUP_WIDTH
    xs0, bm0, cm0 = BLK_XS * COL_BLOCK // gw, BLK_BM * COL_BLOCK // SSD_STATE, BLK_CM * COL_BLOCK // SSD_STATE

    n_pairs = batch * SSD_GROUPS

    def back(k):
        p = jnp.minimum(k, n_pairs - 1)
        return p // SSD_GROUPS, p % SSD_GROUPS

    def out(k):
        p = jnp.maximum(k - 1, 0)
        return p // SSD_GROUPS, p % SSD_GROUPS

    def back_rows(k, i):
        return back(k)[0] * nb + nb - 1 - i

    def out_rows(k, i):
        return out(k)[0] * nb + i

    def dt_specs(pair):
        rows = back_rows if pair is back else out_rows
        grp = (1, DT_ROWS, LANES)
        return [
            pl.BlockSpec((cpb, DT_ROWS, SSD_CHUNK), lambda k, i: (rows(k, i), pair(k)[1], 0)),
            pl.BlockSpec(grp, lambda k, i: (pair(k)[1], 0, 0)),
            pl.BlockSpec(grp, lambda k, i: (pair(k)[1], 0, 0)),
        ]

    const2 = lambda k, i: (0, 0)
    return pl.pallas_call(
        functools.partial(_ssd_kernel, cpb=cpb),
        grid=(n_pairs + 1, nb),
        in_specs=[
            pl.BlockSpec((tq, gw), lambda k, i: (back_rows(k, i), xs0 + back(k)[1])),
            pl.BlockSpec((tq, SSD_STATE), lambda k, i: (back_rows(k, i), bm0 + back(k)[1])),
            *dt_specs(back),
            pl.BlockSpec((tq, gw), lambda k, i: (out_rows(k, i), xs0 + out(k)[1])),
            pl.BlockSpec((tq, SSD_STATE), lambda k, i: (out_rows(k, i), bm0 + out(k)[1])),
            pl.BlockSpec((tq, SSD_STATE), lambda k, i: (out_rows(k, i), cm0 + out(k)[1])),
            pl.BlockSpec((tq, gw), lambda k, i: (out_rows(k, i), out(k)[1])),
            *dt_specs(out),
            pl.BlockSpec((1, gw), lambda k, i: (0, out(k)[1])),
            pl.BlockSpec((1, gw), lambda k, i: (0, out(k)[1])),
            pl.BlockSpec((SSD_CHUNK, TRI_BLOCKS * SSD_CHUNK), const2),
            pl.BlockSpec((SSD_CHUNK, EXPAND_BLOCKS * gw), const2),
        ],
        out_specs=pl.BlockSpec((tq, gw), lambda k, i: (jnp.where(k == 0, t // tq, out_rows(k, i)),
                                                       jnp.where(k == 0, i % SSD_GROUPS, out(k)[1]))),
        out_shape=jax.ShapeDtypeStruct((t + tq, SSD_WIDTH), BF16),
        scratch_shapes=[
            pltpu.VMEM((2, seq_len // SSD_CHUNK, SSD_STATE, gw), BF16),
            pltpu.VMEM((SSD_STATE, gw), F32),
            pltpu.VMEM((SSD_STATE, gw), F32),
        ],
        compiler_params=pltpu.CompilerParams(
            dimension_semantics=("arbitrary", "arbitrary"),
            vmem_limit_bytes=48 * 1024 * 1024),
        name="ssd",
    )(proj, proj, dt4, dtb, alog, proj, proj, proj, proj, dt4, dtb, alog, dskip, ng, tri, expand)


def _memkv_kernel(mem_ref, g_ref, w_ref, kv_ref):
    m = _rms(mem_ref[0], g_ref[...]).astype(BF16)
    kv_ref[0] = jnp.dot(m, w_ref[...], preferred_element_type=F32).astype(BF16)


def _memkv(mem, g, w_kv):
    b = mem.shape[0]
    return pl.pallas_call(
        _memkv_kernel,
        grid=(b,),
        in_specs=[
            pl.BlockSpec((1, MEM_LEN, D_MODEL), lambda i: (i, 0, 0)),
            pl.BlockSpec((1, D_MODEL), lambda i: (0, 0)),
            pl.BlockSpec((D_MODEL, 2 * D_MODEL), lambda i: (0, 0)),
        ],
        out_specs=pl.BlockSpec((1, MEM_LEN, 2 * D_MODEL), lambda i: (i, 0, 0)),
        out_shape=jax.ShapeDtypeStruct((b, MEM_LEN, 2 * D_MODEL), BF16),
        compiler_params=pltpu.CompilerParams(dimension_semantics=("parallel",)),
        name="memkv",
    )(mem, g, w_kv)


def _tail_kernel(yssd_ref, gate_ref, u_ref, v_ref, xq_ref, xgate_ref, m0_ref, m1_ref, m2_ref, x_ref,
                 k_ref, vmem_ref, ws_ref, bexp_ref, wbs_ref, wbg_ref, wbx_ref, wout_ref, gpost_ref,
                 out_ref, sv_scr, o_scr, *, tq):
    n_chunks = tq // GMLP_CHUNK
    gc = GMLP_CHUNK
    gwid = GMLP_GROUP_WIDTH

    for g in range(GMLP_GROUPS):
        cols = slice(g * gwid, (g + 1) * gwid)
        vcat = jnp.concatenate([v_ref[c * gc:(c + 1) * gc, cols] for c in range(n_chunks)], axis=1)
        sv = jnp.dot(ws_ref[g], vcat, preferred_element_type=F32)
        for c in range(n_chunks):
            sv_scr[c * gc:(c + 1) * gc, cols] = sv[:, c * gc:(c + 1) * gc]

    nt_dims = (((1,), (1,)), ((), ()))
    scale = XATTN_HEAD_DIM ** -0.5
    heads = [slice(h * XATTN_HEAD_DIM, (h + 1) * XATTN_HEAD_DIM) for h in range(XATTN_HEADS)]
    scores = [lax.dot_general(xq_ref[:, hc], k_ref[0, :, hc], nt_dims, preferred_element_type=F32) * scale
              for hc in heads]
    ssd_proj = jnp.dot(yssd_ref[...], wbs_ref[...], preferred_element_type=F32)

    for c in range(n_chunks):
        rows = slice(c * gc, (c + 1) * gc)
        sv_scr[rows, :] = (u_ref[rows, :].astype(F32) * (sv_scr[rows, :] + bexp_ref[...])
                           * gate_ref[rows, :].astype(F32))

    probs = []
    for s in scores:
        e = jnp.exp(s - jnp.max(s, axis=-1, keepdims=True))
        probs.append((e / jnp.sum(e, axis=-1, keepdims=True)).astype(BF16))
    for hc, p in zip(heads, probs):
        o = jnp.dot(p, vmem_ref[0, :, hc], preferred_element_type=F32)
        o_scr[:, hc] = o * xgate_ref[:, hc].astype(F32)

    merged = m0_ref[...].astype(F32) * ssd_proj
    merged = merged + m1_ref[...].astype(F32) * jnp.dot(sv_scr[...].astype(BF16), wbg_ref[...],
                                                        preferred_element_type=F32)
    merged = merged + m2_ref[...].astype(F32) * jnp.dot(o_scr[...].astype(BF16), wbx_ref[...],
                                                        preferred_element_type=F32)
    out = jnp.dot(merged.astype(BF16), wout_ref[...], preferred_element_type=F32)
    out_ref[...] = x_ref[...] + _rms(out, gpost_ref[...])


def _tail(y_ssd, proj, x2, kv, ws, bexp, wbs, wbg, wbx, wout, gpost, *, seq_len, tq=512):
    t = x2.shape[0]
    assert seq_len % tq == 0 and tq % GMLP_CHUNK == 0
    per_seq = seq_len // tq
    resident = functools.partial(pl.BlockSpec, pipeline_mode=pl.Buffered(1))

    def proj_block(blk):
        return pl.BlockSpec((tq, COL_BLOCK), lambda i: (i, blk))

    return pl.pallas_call(
        functools.partial(_tail_kernel, tq=tq),
        grid=(t // tq,),
        in_specs=[
            pl.BlockSpec((tq, SSD_WIDTH), lambda i: (i, 0)),
            proj_block(BLK_GATE), proj_block(BLK_U), proj_block(BLK_V), proj_block(BLK_XQ), proj_block(BLK_XGATE),
            proj_block(BLK_MERGE), proj_block(BLK_MERGE + 1), proj_block(BLK_MERGE + 2),
            pl.BlockSpec((tq, D_MODEL), lambda i: (i, 0)),
            pl.BlockSpec((1, MEM_LEN, D_MODEL), lambda i: (i // per_seq, 0, 0)),
            pl.BlockSpec((1, MEM_LEN, D_MODEL), lambda i: (i // per_seq, 0, 1)),
            resident((GMLP_GROUPS, GMLP_CHUNK, GMLP_CHUNK), lambda i: (0, 0, 0)),
            resident((GMLP_CHUNK, GMLP_WIDTH), lambda i: (0, 0)),
            resident((SSD_WIDTH, D_MODEL), lambda i: (0, 0)),
            resident((GMLP_WIDTH, D_MODEL), lambda i: (0, 0)),
            resident((D_MODEL, D_MODEL), lambda i: (0, 0)),
            resident((D_MODEL, D_MODEL), lambda i: (0, 0)),
            resident((1, D_MODEL), lambda i: (0, 0)),
        ],
        out_specs=pl.BlockSpec((tq, D_MODEL), lambda i: (i, 0)),
        out_shape=jax.ShapeDtypeStruct((t, D_MODEL), F32),
        scratch_shapes=[
            pltpu.VMEM((tq, GMLP_WIDTH), F32),
            pltpu.VMEM((tq, D_MODEL), F32),
        ],
        compiler_params=pltpu.CompilerParams(
            dimension_semantics=("parallel",),
            vmem_limit_bytes=56 * 1024 * 1024),
        name="tail",
    )(y_ssd, proj, proj, proj, proj, proj, proj, proj, proj, x2, kv, kv,
      ws, bexp, wbs, wbg, wbx, wout, gpost)


def _split_bf16(w):
    hi = w.astype(BF16)
    return hi, (w - hi.astype(F32)).astype(BF16)


def _prep_layer(p, l):
    w_in = p['w_in'][l]
    dt0 = SSD_WIDTH + SSD_WIDTH + 2 * SSD_GROUPS * SSD_STATE
    dt1 = dt0 + 2 * SSD_HEADS
    w_main = jnp.concatenate([w_in[:, :dt0], w_in[:, dt1:]], axis=1).astype(BF16)
    w_dt = w_in[:, dt0:dt1].reshape(D_MODEL, 2, SSD_GROUPS, HEADS_PER_GROUP)
    w_dt = w_dt.transpose(2, 1, 3, 0).reshape(2 * SSD_HEADS, D_MODEL)
    wdt_hi, wdt_lo = _split_bf16(w_dt)

    def per_group_rows(v):
        v = v.reshape(2, SSD_GROUPS, HEADS_PER_GROUP).transpose(1, 0, 2).reshape(SSD_GROUPS, DT_ROWS, 1)
        return jnp.broadcast_to(v, (SSD_GROUPS, DT_ROWS, LANES))

    return dict(
        dtb=per_group_rows(p['dt_bias'][l]), alog=per_group_rows(p['a_log'][l]),
        dskip=jnp.repeat(p['d_skip'][l], SSD_HEAD_DIM).reshape(1, SSD_WIDTH),
        ssd_ng=p['ssd_norm_g'][l].reshape(1, SSD_WIDTH),
        gpre=p['norm_pre_g'][l].reshape(1, D_MODEL),
        w_main=w_main, wdt_hi=wdt_hi, wdt_lo=wdt_lo,
        convw=p['conv_w'][l], convb=p['conv_b'][l].reshape(1, -1),
        lng=p['gmlp_ln_g'][l].reshape(1, -1), lnb=p['gmlp_ln_b'][l].reshape(1, -1),
        ws=p['w_spatial'][l].astype(BF16),
        bexp=jnp.repeat(p['b_spatial'][l].T, GMLP_GROUP_WIDTH, axis=1),
        mem_g=p['mem_norm_g'][l].reshape(1, D_MODEL), w_kv=p['w_kv'][l].astype(BF16),
        wbs=p['w_br_ssd'][l].astype(BF16), wbg=p['w_br_gmlp'][l].astype(BF16),
        wbx=p['w_br_xattn'][l].astype(BF16), wout=p['w_out'][l].astype(BF16),
        gpost=p['norm_post_g'][l].reshape(1, D_MODEL),
    )


def _layer(x2, mem, prm, *, batch, seq_len):
    proj, dt4 = _inproj(x2, prm['gpre'], prm['w_main'], prm['wdt_hi'], prm['wdt_lo'],
                        prm['convw'], prm['convb'], prm['lng'], prm['lnb'], seq_len=seq_len)
    y_ssd = _ssd(proj, dt4, prm['dtb'], prm['alog'], prm['dskip'], prm['ssd_ng'], batch=batch, seq_len=seq_len)
    kv = _memkv(mem, prm['mem_g'], prm['w_kv'])
    return _tail(y_ssd, proj, x2, kv, prm['ws'], prm['bexp'], prm['wbs'], prm['wbg'], prm['wbx'], prm['wout'],
                 prm['gpost'], seq_len=seq_len)


def kernel(x, mem, norm_pre_g, w_in, conv_w, conv_b, dt_bias, a_log, d_skip, ssd_norm_g, gmlp_ln_g, gmlp_ln_b, w_spatial, b_spatial, mem_norm_g, w_kv, w_br_ssd, w_br_gmlp, w_br_xattn, w_out, norm_post_g):
    p = dict(norm_pre_g=norm_pre_g, w_in=w_in, conv_w=conv_w, conv_b=conv_b, dt_bias=dt_bias, a_log=a_log,
             d_skip=d_skip, ssd_norm_g=ssd_norm_g, gmlp_ln_g=gmlp_ln_g, gmlp_ln_b=gmlp_ln_b, w_spatial=w_spatial,
             b_spatial=b_spatial, mem_norm_g=mem_norm_g, w_kv=w_kv, w_br_ssd=w_br_ssd, w_br_gmlp=w_br_gmlp,
             w_br_xattn=w_br_xattn, w_out=w_out, norm_post_g=norm_post_g)
    batch, seq_len, _ = x.shape
    x2 = x.reshape(batch * seq_len, D_MODEL)
    for l in range(w_in.shape[0]):
        x2 = _layer(x2, mem, _prep_layer(p, l), batch=batch, seq_len=seq_len)
    return x2.reshape(batch, seq_len, D_MODEL)
```
